```python
import jax, jax.numpy as jnp
from jax import lax
import numpy as np

D_MODEL = 4096
BATCH = 4
SEQ = 2048
DEPTH = 2
DEC_BATCH = 8
DEC_SEQ = 8
PAST_LEN = 16384
PAGE_SIZE = 128

HEAD_DIM = 128
ATTN_WIDTH = D_MODEL // 2
N_HEADS = ATTN_WIDTH // HEAD_DIM
CONV_CH = D_MODEL - ATTN_WIDTH
CONV_WIDTH = 31
D_FF = ((8 * D_MODEL // 3 + 127) // 128) * 128
Q_BLOCK = 128
N_MOD = 9
LN_EPS = 1e-5
ALPHA = (2 * DEPTH) ** 0.25
BETA = (8 * DEPTH) ** -0.25
IN_COLS = 3 * ATTN_WIDTH + N_HEADS + 2 * CONV_CH
FORGET_BIAS_LO = 2.0
FORGET_BIAS_HI = 12.0

kernel_name = "fox_conformer_hybrid_adaln_step"


def layer_norm(x, g, b):
    xf = x.astype(jnp.float32)
    mu = jnp.mean(xf, axis=-1, keepdims=True)
    var = jnp.mean(jnp.square(xf - mu), axis=-1, keepdims=True)
    return (xf - mu) * lax.rsqrt(var + LN_EPS) * g + b


def adaln(c, w_ada, b_ada):
    m = jax.nn.silu(c) @ w_ada + b_ada
    return m.reshape(c.shape[0], N_MOD, 1, D_MODEL)


def modulate(x, shift, scale):
    return x * (1.0 + scale) + shift


def swiglu(h, w_in, w_down):
    gt, up = jnp.split(h @ w_in, 2, axis=-1)
    return (jax.nn.silu(gt) * up) @ w_down


def ffn_sublayer(x, shift, scale, gate, w_in, w_down, g, b):
    h = modulate(x, shift, scale)
    return layer_norm(ALPHA * x + 0.5 * gate * swiglu(h, w_in, w_down), g, b)


def mixer_inputs(h, w_in, b_f):
    B, T = h.shape[:2]
    q, k, v, f, glu = jnp.split(h @ w_in, [ATTN_WIDTH, 2 * ATTN_WIDTH, 3 * ATTN_WIDTH,
                                           3 * ATTN_WIDTH + N_HEADS], axis=-1)
    q = q.reshape(B, T, N_HEADS, HEAD_DIM)
    k = k.reshape(B, T, N_HEADS, HEAD_DIM)
    v = v.reshape(B, T, N_HEADS, HEAD_DIM)
    logf = jax.nn.log_sigmoid((f + b_f).astype(jnp.float32))
    a, gl = jnp.split(glu, 2, axis=-1)
    u = a * jax.nn.sigmoid(gl)
    return q, k, v, logf, u


def fox_attention_prompt(q, k, v, logf):
    B, S = q.shape[:2]
    scale = HEAD_DIM ** -0.5
    cum_t = lax.cumsum(logf, axis=1).transpose(0, 2, 1)
    kpos = jnp.arange(S)

    def block(i):
        start = i * Q_BLOCK
        qb = lax.dynamic_slice_in_dim(q, start, Q_BLOCK, axis=1)
        cb = lax.dynamic_slice_in_dim(cum_t, start, Q_BLOCK, axis=2)
        s = jnp.einsum('bqhd,bkhd->bhqk', qb, k).astype(jnp.float32) * scale
        bias = cb[..., :, None] - cum_t[..., None, :]
        qpos = start + jnp.arange(Q_BLOCK)
        causal = kpos[None, :] <= qpos[:, None]
        p = jax.nn.softmax(jnp.where(causal, s + bias, -jnp.inf), axis=-1)
        return jnp.einsum('bhqk,bkhd->bqhd', p.astype(v.dtype), v)

    out = lax.map(block, jnp.arange(S // Q_BLOCK))
    return out.transpose(1, 0, 2, 3, 4).reshape(B, S, ATTN_WIDTH)


def fox_attention_sample(q, k_new, v_new, logf_new, k_past, v_past, logf_past):
    B, T = q.shape[:2]
    scale = HEAD_DIM ** -0.5
    logf_past = logf_past.astype(jnp.float32)
    c_new = lax.cumsum(logf_new, axis=1).transpose(0, 2, 1)
    r_past = (lax.cumsum(logf_past, axis=1, reverse=True) - logf_past).transpose(0, 2, 1)
    s_past = jnp.einsum('bqhd,bkhd->bhqk', q, k_past).astype(jnp.float32) * scale
    s_past = s_past + c_new[..., :, None] + r_past[..., None, :]
    s_new = jnp.einsum('bqhd,bkhd->bhqk', q, k_new).astype(jnp.float32) * scale
    s_new = s_new + c_new[..., :, None] - c_new[..., None, :]
    causal = jnp.tril(jnp.ones((T, T), dtype=bool))
    s_new = jnp.where(causal, s_new, -jnp.inf)
    P = k_past.shape[1]
    p = jax.nn.softmax(jnp.concatenate([s_past, s_new], axis=-1), axis=-1)
    out = (jnp.einsum('bhqk,bkhd->bqhd', p[..., :P].astype(v_past.dtype), v_past)
           + jnp.einsum('bhqk,bkhd->bqhd', p[..., P:].astype(v_new.dtype), v_new))
    return out.reshape(B, T, ATTN_WIDTH)


def conv_module(u_ext, w_dw, b_dw, g_cn, b_cn):
    y = lax.conv_general_dilated(u_ext, w_dw[:, None, :].astype(u_ext.dtype), window_strides=(1,),
                                 padding='VALID', dimension_numbers=('NWC', 'WIO', 'NWC'),
                                 feature_group_count=CONV_CH)
    return jax.nn.silu(layer_norm(y + b_dw, g_cn, b_cn))


def merge_sublayer(x, attn, u_ext, gate, w_dw, b_dw, g_cn, b_cn, w_out, g, b):
    conv = conv_module(u_ext, w_dw, b_dw, g_cn, b_cn)
    out = jnp.concatenate([attn, conv.astype(attn.dtype)], axis=-1) @ w_out
    return layer_norm(ALPHA * x + gate * out, g, b)


def setup_inputs(seed: int = 0) -> dict:
    key = jax.random.key(seed)
    ks = jax.random.split(key, 24)
    n_pages = PAST_LEN // PAGE_SIZE
    n_used = DEC_BATCH * n_pages
    n_pool = (n_used * 5) // 4
    f32 = jnp.float32
    nrm = lambda k, shape, s=1.0: jax.random.normal(k, shape, f32) * s
    page_table = jax.random.permutation(ks[8], n_pool)[:n_used].reshape(DEC_BATCH, n_pages).astype(jnp.int32)
    b_f = jnp.linspace(FORGET_BIAS_LO, FORGET_BIAS_HI, N_HEADS, dtype=f32)[None, :] + nrm(ks[14], (DEPTH, N_HEADS), 0.1)
    cache_logf = jax.nn.log_sigmoid(b_f[:, None, None, :] + nrm(ks[6], (DEPTH, n_pool, PAGE_SIZE, N_HEADS), 0.5))
    return {
        "x_prompt": nrm(ks[0], (BATCH, SEQ, D_MODEL)),
        "x_sample": nrm(ks[1], (DEC_BATCH, DEC_SEQ, D_MODEL)),
        "c_prompt": nrm(ks[2], (BATCH, D_MODEL)),
        "c_sample": nrm(ks[3], (DEC_BATCH, D_MODEL)),
        "cache_k": nrm(ks[4], (DEPTH, n_pool, PAGE_SIZE, N_HEADS, HEAD_DIM)),
        "cache_v": nrm(ks[5], (DEPTH, n_pool, PAGE_SIZE, N_HEADS, HEAD_DIM)),
        "cache_logf": cache_logf,
        "state_conv": nrm(ks[7], (DEPTH, DEC_BATCH, CONV_WIDTH - 1, CONV_CH), 0.5),
        "page_table": page_table,
        "w_ada": nrm(ks[9], (DEPTH, D_MODEL, N_MOD * D_MODEL), D_MODEL ** -0.5),
        "b_ada": nrm(ks[10], (DEPTH, N_MOD * D_MODEL), 0.01),
        "w_ffn1_in": nrm(ks[11], (DEPTH, D_MODEL, 2 * D_FF), D_MODEL ** -0.5),
        "w_ffn1_down": nrm(ks[12], (DEPTH, D_FF, D_MODEL), BETA * D_FF ** -0.5),
        "w_in": nrm(ks[13], (DEPTH, D_MODEL, IN_COLS), D_MODEL ** -0.5),
        "b_f": b_f,
        "w_dw": nrm(ks[15], (DEPTH, CONV_WIDTH, CONV_CH), CONV_WIDTH ** -0.5),
        "b_dw": nrm(ks[16], (DEPTH, CONV_CH), 0.01),
        "g_cn": 1.0 + nrm(ks[17], (DEPTH, CONV_CH), 0.01),
        "b_cn": nrm(ks[18], (DEPTH, CONV_CH), 0.01),
        "w_out": nrm(ks[19], (DEPTH, D_MODEL, D_MODEL), BETA * D_MODEL ** -0.5),
        "w_ffn2_in": nrm(ks[20], (DEPTH, D_MODEL, 2 * D_FF), D_MODEL ** -0.5),
        "w_ffn2_down": nrm(ks[21], (DEPTH, D_FF, D_MODEL), BETA * D_FF ** -0.5),
        "ln_g": 1.0 + nrm(ks[22], (DEPTH, 3, D_MODEL), 0.01),
        "ln_b": nrm(ks[23], (DEPTH, 3, D_MODEL), 0.01),
    }


def reference(x_prompt, x_sample, c_prompt, c_sample, cache_k, cache_v, cache_logf, state_conv, page_table,
              w_ada, b_ada, w_ffn1_in, w_ffn1_down, w_in, b_f, w_dw, b_dw, g_cn, b_cn, w_out,
              w_ffn2_in, w_ffn2_down, ln_g, ln_b):
    xp, xs = x_prompt, x_sample
    kp_l, vp_l, lfp_l, cvp_l = [], [], [], []
    ks_l, vs_l, lfs_l, cvs_l = [], [], [], []
    n_sample = page_table.shape[0]
    for l in range(DEPTH):
        mp = adaln(c_prompt, w_ada[l], b_ada[l])
        ms = adaln(c_sample, w_ada[l], b_ada[l])

        xp = ffn_sublayer(xp, mp[:, 0], mp[:, 1], mp[:, 2], w_ffn1_in[l], w_ffn1_down[l], ln_g[l, 0], ln_b[l, 0])
        xs = ffn_sublayer(xs, ms[:, 0], ms[:, 1], ms[:, 2], w_ffn1_in[l], w_ffn1_down[l], ln_g[l, 0], ln_b[l, 0])

        qp, kp, vp, lfp, up = mixer_inputs(modulate(xp, mp[:, 3], mp[:, 4]), w_in[l], b_f[l])
        attn_p = fox_attention_prompt(qp, kp, vp, lfp)
        up_ext = jnp.pad(up, ((0, 0), (CONV_WIDTH - 1, 0), (0, 0)))
        xp = merge_sublayer(xp, attn_p, up_ext, mp[:, 5], w_dw[l], b_dw[l], g_cn[l], b_cn[l], w_out[l],
                            ln_g[l, 1], ln_b[l, 1])
        kp_l.append(kp); vp_l.append(vp); lfp_l.append(lfp); cvp_l.append(up_ext[:, -(CONV_WIDTH - 1):])

        qs, kn, vn, lfn, us = mixer_inputs(modulate(xs, ms[:, 3], ms[:, 4]), w_in[l], b_f[l])
        k_past = cache_k[l][page_table].reshape(n_sample, -1, N_HEADS, HEAD_DIM)
        v_past = cache_v[l][page_table].reshape(n_sample, -1, N_HEADS, HEAD_DIM)
        lf_past = cache_logf[l][page_table].reshape(n_sample, -1, N_HEADS)
        attn_s = fox_attention_sample(qs, kn, vn, lfn, k_past, v_past, lf_past)
        us_ext = jnp.concatenate([state_conv[l].astype(us.dtype), us], axis=1)
        xs = merge_sublayer(xs, attn_s, us_ext, ms[:, 5], w_dw[l], b_dw[l], g_cn[l], b_cn[l], w_out[l],
                            ln_g[l, 1], ln_b[l, 1])
        ks_l.append(kn); vs_l.append(vn); lfs_l.append(lfn); cvs_l.append(us_ext[:, -(CONV_WIDTH - 1):])

        xp = ffn_sublayer(xp, mp[:, 6], mp[:, 7], mp[:, 8], w_ffn2_in[l], w_ffn2_down[l], ln_g[l, 2], ln_b[l, 2])
        xs = ffn_sublayer(xs, ms[:, 6], ms[:, 7], ms[:, 8], w_ffn2_in[l], w_ffn2_down[l], ln_g[l, 2], ln_b[l, 2])

    return (xp, xs,
            jnp.stack(kp_l), jnp.stack(vp_l), jnp.stack(lfp_l), jnp.stack(cvp_l),
            jnp.stack(ks_l), jnp.stack(vs_l), jnp.stack(lfs_l), jnp.stack(cvs_l))
```

```python
import functools

import jax
import jax.numpy as jnp
from jax import lax
from jax.experimental import pallas as pl
from jax.experimental.pallas import tpu as pltpu

F32 = jnp.float32
BF16 = jnp.bfloat16
LN_EPS = 1e-5
N_MOD = 9
LANES = 128
SUBLANES = 8
VMEM_LIMIT_BYTES = 56 * 1024 * 1024


def _params(*sem):
    return pltpu.CompilerParams(dimension_semantics=sem, vmem_limit_bytes=VMEM_LIMIT_BYTES)


def _split3(x):
    hi = x.astype(BF16)
    r1 = x - hi.astype(F32)
    mid = r1.astype(BF16)
    lo = (r1 - mid.astype(F32)).astype(BF16)
    return hi, mid, lo


def _dot01_left(m01, x):
    return sum(jnp.dot(m01, t, preferred_element_type=F32) for t in _split3(x))


def _dot01_right(x, m01):
    return sum(jnp.dot(t, m01, preferred_element_type=F32) for t in _split3(x))


def _adaln_kernel(c_ref, w_ref, b_ref, o_ref):
    c = c_ref[...]
    x = (c * jax.nn.sigmoid(c)).astype(BF16)
    o_ref[...] = jnp.dot(x, w_ref[...].astype(BF16), preferred_element_type=F32) + b_ref[...]


def _adaln(c_all, w_ada, b_ada3, layer, tn):
    rows, d = c_all.shape
    n = w_ada.shape[2]
    return pl.pallas_call(
        _adaln_kernel,
        grid=(n // tn,),
        in_specs=[pl.BlockSpec((rows, d), lambda j: (0, 0)),
                  pl.BlockSpec((None, d, tn), lambda j: (layer, 0, j)),
                  pl.BlockSpec((None, 1, tn), lambda j: (layer, 0, j))],
        out_specs=pl.BlockSpec((rows, tn), lambda j: (0, j)),
        out_shape=jax.ShapeDtypeStruct((rows, n), F32),
        compiler_params=_params("arbitrary"),
        name="adaln",
    )(c_all, w_ada, b_ada3)


def _epi_plain(accs, bias):
    return [accs[0]]


def _epi_swiglu(accs, bias):
    g, u = accs
    return [g * jax.nn.sigmoid(g) * u]


def _epi_glu(accs, bias):
    a, gl = accs
    return [a * jax.nn.sigmoid(gl)]


def _epi_logsig(accs, bias):
    z = accs[0] + bias
    return [jnp.minimum(z, 0.0) - jnp.log1p(jnp.exp(-jnp.abs(z)))]


def _mm_kernel(*refs, n_x, n_br, n_out, n_p, has_bias, epilogue):
    it = iter(refs)
    x_p = [next(it) for _ in range(n_x)]
    x_s = [next(it) for _ in range(n_x)]
    w = [[next(it) for _ in range(n_x)] for _ in range(n_br)]
    bias = next(it) if has_bias else None
    o_p = [next(it) for _ in range(n_out)]
    o_s = [next(it) for _ in range(n_out)]
    wb = [[next(it) for _ in range(n_x)] for _ in range(n_br)]
    i = pl.program_id(1)

    @pl.when(i == 0)
    def _():
        for b in range(n_br):
            for p in range(n_x):
                wb[b][p][...] = w[b][p][...].astype(BF16)

    def run(xs, outs):
        accs = []
        for b in range(n_br):
            acc = None
            for p in range(n_x):
                d = jnp.dot(xs[p][...], wb[b][p][...], preferred_element_type=F32)
                acc = d if acc is None else acc + d
            accs.append(acc)
        vals = epilogue(accs, bias[...] if has_bias else None)
        for o, v in zip(outs, vals):
            o[...] = v.astype(o.dtype)

    @pl.when(i < n_p)
    def _():
        run(x_p, o_p)

    @pl.when(i == n_p)
    def _():
        run(x_s, o_s)


def _mm(xs_p, xs_s, w_specs, tn, n_cols, epilogue, out_dtypes, tm, layer, bias=None, name="mm"):
    n_x, n_br, n_out = len(xs_p), len(w_specs), len(out_dtypes)
    m_p, m_s = xs_p[0].shape[0], xs_s[0].shape[0]
    n_p = m_p // tm
    grid = (n_cols // tn, n_p + 1)
    in_specs, args = [], []
    for x in xs_p:
        in_specs.append(pl.BlockSpec((tm, x.shape[1]), lambda j, i: (jnp.minimum(i, n_p - 1), 0)))
        args.append(x)
    for x in xs_s:
        in_specs.append(pl.BlockSpec((m_s, x.shape[1]), lambda j, i: (0, 0)))
        args.append(x)
    scratch = []
    for b in range(n_br):
        for p in range(n_x):
            w, rb, cb = w_specs[b][p]
            kp = xs_p[p].shape[1]
            in_specs.append(pl.BlockSpec((None, kp, tn), lambda j, i, rb=rb, cb=cb: (layer, rb, j + cb)))
            args.append(w)
            scratch.append(pltpu.VMEM((kp, tn), BF16))
    if bias is not None:
        in_specs.append(pl.BlockSpec((None, 1, tn), lambda j, i: (layer, 0, j)))
        args.append(bias)
    out_specs = ([pl.BlockSpec((tm, tn), lambda j, i: (jnp.minimum(i, n_p - 1), j)) for _ in out_dtypes]
                 + [pl.BlockSpec((m_s, tn), lambda j, i: (0, j)) for _ in out_dtypes])
    out_shape = ([jax.ShapeDtypeStruct((m_p, n_cols), dt) for dt in out_dtypes]
                 + [jax.ShapeDtypeStruct((m_s, n_cols), dt) for dt in out_dtypes])
    outs = pl.pallas_call(
        functools.partial(_mm_kernel, n_x=n_x, n_br=n_br, n_out=n_out, n_p=n_p,
                          has_bias=bias is not None, epilogue=epilogue),
        grid=grid, in_specs=in_specs, out_specs=out_specs, out_shape=out_shape,
        scratch_shapes=scratch,
        compiler_params=_params("arbitrary", "arbitrary"),
        name=name,
    )(*args)
    return [(outs[k], outs[n_out + k]) for k in range(n_out)]


def _mod_kernel(x_p, x_s, sh_p, sc_p, sh_s, sc_s, o_p, o_s, *, n_p):
    i = pl.program_id(0)

    @pl.when(i < n_p)
    def _():
        o_p[...] = (x_p[...] * (1.0 + sc_p[...]) + sh_p[...]).astype(o_p.dtype)

    @pl.when(i == n_p)
    def _():
        o_s[...] = (x_s[...] * (1.0 + sc_s[...]) + sh_s[...]).astype(o_s.dtype)


def _mod_specs(n_p, steps_per_seq, which, d, m_s):
    p = pl.BlockSpec((None, None, 1, d),
                     lambda i, w=which: (jnp.minimum(i, n_p - 1) // steps_per_seq, w, 0, 0))
    s = pl.BlockSpec((None, m_s, d), lambda i, w=which: (w, 0, 0))
    return p, s


def _modulate(x_p, x_s, mod_p, mod_s, which_shift, tr, seq):
    m_p, d = x_p.shape
    m_s = x_s.shape[0]
    n_p = m_p // tr
    spp = seq // tr
    sh_p, sh_s = _mod_specs(n_p, spp, which_shift, d, m_s)
    sc_p, sc_s = _mod_specs(n_p, spp, which_shift + 1, d, m_s)
    row_p = pl.BlockSpec((tr, d), lambda i: (jnp.minimum(i, n_p - 1), 0))
    row_s = pl.BlockSpec((m_s, d), lambda i: (0, 0))
    return pl.pallas_call(
        functools.partial(_mod_kernel, n_p=n_p),
        grid=(n_p + 1,),
        in_specs=[row_p, row_s, sh_p, sc_p, sh_s, sc_s],
        out_specs=[row_p, row_s],
        out_shape=[jax.ShapeDtypeStruct((m_p, d), BF16), jax.ShapeDtypeStruct((m_s, d), BF16)],
        compiler_params=_params("arbitrary"),
        name="modulate",
    )(x_p, x_s, mod_p, mod_p, mod_s, mod_s)


def _ln_kernel(*refs, n_p, alpha, coef, with_mod):
    if with_mod:
        (x_p, y_p, x_s, y_s, g_p, g_s, sh_p, sc_p, sh_s, sc_s, lng, lnb, o_p, o_s, m_p, m_s) = refs
    else:
        (x_p, y_p, x_s, y_s, g_p, g_s, lng, lnb, o_p, o_s) = refs
    i = pl.program_id(0)

    def run(x, y, gate, shift, scale, o, mo):
        z = alpha * x[...] + coef * gate[...] * y[...]
        mu = jnp.mean(z, axis=-1, keepdims=True)
        zc = z - mu
        var = jnp.mean(zc * zc, axis=-1, keepdims=True)
        out = zc * lax.rsqrt(var + LN_EPS) * lng[...] + lnb[...]
        o[...] = out
        if with_mod:
            mo[...] = (out * (1.0 + scale[...]) + shift[...]).astype(mo.dtype)

    @pl.when(i < n_p)
    def _():
        run(x_p, y_p, g_p, sh_p if with_mod else None, sc_p if with_mod else None, o_p,
            m_p if with_mod else None)

    @pl.when(i == n_p)
    def _():
        run(x_s, y_s, g_s, sh_s if with_mod else None, sc_s if with_mod else None, o_s,
            m_s if with_mod else None)


def _ln_mod(x, y, mod_p, mod_s, which_gate, coef, alpha, ln_g, ln_b, layer, which_ln, next_mod, tr, seq):
    (x_p, x_s), (y_p, y_s) = x, y
    m_p, d = x_p.shape
    m_s = x_s.shape[0]
    n_p = m_p // tr
    spp = seq // tr
    row_p = pl.BlockSpec((tr, d), lambda i: (jnp.minimum(i, n_p - 1), 0))
    row_s = pl.BlockSpec((m_s, d), lambda i: (0, 0))
    g_p, g_s = _mod_specs(n_p, spp, which_gate, d, m_s)
    ln_spec = pl.BlockSpec((None, None, 1, d), lambda i: (layer, which_ln, 0, 0))
    in_specs = [row_p, row_p, row_s, row_s, g_p, g_s]
    args = [x_p, y_p, x_s, y_s, mod_p, mod_s]
    out_specs = [row_p, row_s]
    out_shape = [jax.ShapeDtypeStruct((m_p, d), F32), jax.ShapeDtypeStruct((m_s, d), F32)]
    if next_mod is not None:
        nm_p, nm_s, ws = next_mod
        sh_p, sh_s = _mod_specs(n_p, spp, ws, d, m_s)
        sc_p, sc_s = _mod_specs(n_p, spp, ws + 1, d, m_s)
        in_specs += [sh_p, sc_p, sh_s, sc_s]
        args += [nm_p, nm_p, nm_s, nm_s]
        out_specs += [row_p, row_s]
        out_shape += [jax.ShapeDtypeStruct((m_p, d), BF16), jax.ShapeDtypeStruct((m_s, d), BF16)]
    in_specs += [ln_spec, ln_spec]
    args += [ln_g, ln_b]
    outs = pl.pallas_call(
        functools.partial(_ln_kernel, n_p=n_p, alpha=alpha, coef=coef, with_mod=next_mod is not None),
        grid=(n_p + 1,), in_specs=in_specs, out_specs=out_specs, out_shape=out_shape,
        compiler_params=_params("arbitrary"),
        name="ln_mod",
    )(*args)
    if next_mod is None:
        return (outs[0], outs[1]), None
    return (outs[0], outs[1]), (outs[2], outs[3])


def _pattn_kernel(q_ref, k_ref, v_ref, lf_ref, o_ref, kb, vb, cum, cum_t, *, tq, seq, scale):
    h = pl.program_id(1)
    qi = pl.program_id(2)
    n_blk = seq // tq

    @pl.when(qi == 0)
    def _():
        kb[...] = k_ref[...].astype(BF16)
        vb[...] = v_ref[...].astype(BF16)

    @pl.when((qi == 0) & (h == 0))
    def _():
        r = lax.broadcasted_iota(jnp.int32, (LANES, LANES), 0)
        c = lax.broadcasted_iota(jnp.int32, (LANES, LANES), 1)
        tri = (r >= c).astype(BF16)
        carry = jnp.zeros((1, LANES), F32)
        for blk in range(seq // LANES):
            x = lf_ref[blk * LANES:(blk + 1) * LANES, :]
            cb = _dot01_left(tri, x) + carry
            cum[blk * LANES:(blk + 1) * LANES, :] = cb
            carry = cb[LANES - 1:LANES, :]
        for blk in range(n_blk):
            cum_t[blk] = cum[blk * tq:(blk + 1) * tq, :].T

    q = q_ref[...]
    lane = lax.broadcasted_iota(jnp.int32, (tq, LANES), 1)
    cq = jnp.sum(jnp.where(lane == h, cum[pl.ds(pl.multiple_of(qi * tq, tq), tq), :], 0.0),
                 axis=1, keepdims=True)

    def block(ki, carry, masked):
        m, l, acc = carry
        start = pl.multiple_of(ki * tq, tq)
        s = lax.dot_general(q, kb[pl.ds(start, tq), :], (((1,), (1,)), ((), ())),
                            preferred_element_type=F32) * scale
        ck = cum_t[ki, pl.ds(h, 1), :]
        s = s + (cq - ck)
        if masked:
            row = lax.broadcasted_iota(jnp.int32, (tq, tq), 0)
            col = lax.broadcasted_iota(jnp.int32, (tq, tq), 1)
            s = jnp.where(col <= row, s, -jnp.inf)
        m_new = jnp.maximum(m, jnp.max(s, axis=1, keepdims=True))
        a = jnp.exp(m - m_new)
        p = jnp.exp(s - m_new)
        l = a * l + jnp.sum(p, axis=1, keepdims=True)
        acc = a * acc + jnp.dot(p.astype(BF16), vb[pl.ds(start, tq), :], preferred_element_type=F32)
        return m_new, l, acc

    init = (jnp.full((tq, 1), -jnp.inf, F32), jnp.zeros((tq, 1), F32), jnp.zeros((tq, q_ref.shape[1]), F32))
    carry = lax.fori_loop(0, qi, lambda ki, cr: block(ki, cr, False), init)
    m, l, acc = block(qi, carry, True)
    o_ref[...] = (acc / l).astype(o_ref.dtype)


def _prompt_attention(q_p, k_p, v_p, lf_p, batch, seq, n_heads, head_dim, tq):
    m_p = q_p.shape[0]
    nq = seq // tq
    return pl.pallas_call(
        functools.partial(_pattn_kernel, tq=tq, seq=seq, scale=head_dim ** -0.5),
        grid=(batch, n_heads, nq),
        in_specs=[pl.BlockSpec((tq, head_dim), lambda b, h, qi: (b * nq + qi, h)),
                  pl.BlockSpec((seq, head_dim), lambda b, h, qi: (b, h)),
                  pl.BlockSpec((seq, head_dim), lambda b, h, qi: (b, h)),
                  pl.BlockSpec((seq, LANES), lambda b, h, qi: (b, 0))],
        out_specs=pl.BlockSpec((tq, head_dim), lambda b, h, qi: (b * nq + qi, h)),
        out_shape=jax.ShapeDtypeStruct((m_p, n_heads * head_dim), BF16),
        scratch_shapes=[pltpu.VMEM((seq, head_dim), BF16), pltpu.VMEM((seq, head_dim), BF16),
                        pltpu.VMEM((seq, LANES), F32), pltpu.VMEM((nq, LANES, tq), F32)],
        compiler_params=_params("arbitrary", "arbitrary", "arbitrary"),
        name="prompt_attention",
    )(q_p, k_p, v_p, lf_p)


def _sattn_kernel(pt_ref, wq_ref, kn_ref, vn_ref, lfn_ref, *refs, g_pages, page, n_heads, t_new, scale):
    k_refs = refs[:g_pages]
    v_refs = refs[g_pages:2 * g_pages]
    lf_refs = refs[2 * g_pages:3 * g_pages]
    o_ref = refs[3 * g_pages]
    kb, vb, lfbuf, crow_s, m_s, l_s, acc_s, carry_s = refs[3 * g_pages + 1:]
    st = pl.program_id(1)
    n_st = pl.num_programs(1)
    ht = n_heads * t_new
    hd = kb.shape[1]
    d = hd // n_heads
    wq = wq_ref[...]

    expand = (lax.broadcasted_iota(jnp.int32, (LANES, ht), 0)
              == lax.broadcasted_iota(jnp.int32, (LANES, ht), 1) // t_new).astype(BF16)
    key_l = lax.broadcasted_iota(jnp.int32, (page, LANES), 0)
    key_q = lax.broadcasted_iota(jnp.int32, (page, ht), 0)
    lane_t = lax.broadcasted_iota(jnp.int32, (page, ht), 1) % t_new

    def update(s_t, v_chunk):
        m_old = m_s[0:1, :]
        m_new = jnp.maximum(m_old, jnp.max(s_t, axis=0, keepdims=True))
        a = jnp.exp(m_old - m_new)
        p = jnp.exp(s_t - m_new)
        l_s[...] = jnp.broadcast_to(a * l_s[0:1, :] + jnp.sum(p, axis=0, keepdims=True), l_s.shape)
        m_s[...] = jnp.broadcast_to(m_new, m_s.shape)
        o_full = jnp.dot(p.T.astype(BF16), v_chunk, preferred_element_type=F32)
        o_diag = jnp.concatenate(
            [o_full[hh * t_new:(hh + 1) * t_new, hh * d:(hh + 1) * d] for hh in range(n_heads)], axis=0)
        a_col = jnp.broadcast_to(a, (ht, ht)).T[:, 0:1]
        acc_s[...] = a_col * acc_s[...] + o_diag

    @pl.when(st == 0)
    def _():
        lfbuf[...] = jnp.zeros(lfbuf.shape, F32)
        carry_s[...] = jnp.zeros(carry_s.shape, F32)
        m_s[...] = jnp.full(m_s.shape, -jnp.inf, F32)
        l_s[...] = jnp.zeros(l_s.shape, F32)
        acc_s[...] = jnp.zeros(acc_s.shape, F32)
        pad = jnp.zeros((page - t_new, hd), F32)
        kn = jnp.concatenate([kn_ref[...], pad], axis=0).astype(BF16)
        vn = jnp.concatenate([vn_ref[...], pad], axis=0).astype(BF16)
        lfbuf[0:t_new, 0:n_heads] = lfn_ref[...]
        tri = (lax.broadcasted_iota(jnp.int32, (page, page), 0)
               >= lax.broadcasted_iota(jnp.int32, (page, page), 1)).astype(BF16)
        c_nat = _dot01_left(tri, lfbuf[0:page, :])
        c_exp = _dot01_right(c_nat, expand)
        c_row = jnp.sum(jnp.where(key_q == lane_t, c_exp, 0.0), axis=0, keepdims=True)
        crow_s[...] = jnp.broadcast_to(c_row, crow_s.shape)
        s_t = jnp.dot(kn, wq, preferred_element_type=F32) * scale
        s_t = s_t + (c_row - c_exp)
        s_t = jnp.where(key_q <= lane_t, s_t, -jnp.inf)
        update(s_t, vn)

    c_row = crow_s[0:1, :]
    for g in range(g_pages):
        kb[g * page:(g + 1) * page, :] = k_refs[g][...].astype(BF16)
        vb[g * page:(g + 1) * page, :] = v_refs[g][...].astype(BF16)
        lfbuf[g * page:(g + 1) * page, 0:n_heads] = lf_refs[g][...].astype(F32)
    carry = carry_s[0:1, :]
    r_parts = [None] * g_pages
    for g in reversed(range(g_pages)):
        x = lfbuf[g * page:(g + 1) * page, :]
        y = x
        shift = 1
        while shift < page:
            y = y + jnp.where(key_l + shift < page, pltpu.roll(y, page - shift, axis=0), 0.0)
            shift *= 2
        excl = jnp.where(key_l + 1 < page, pltpu.roll(y, page - 1, axis=0), 0.0)
        r_parts[g] = excl + carry
        carry = carry + y[0:1, :]
    carry_s[...] = jnp.broadcast_to(carry, carry_s.shape)
    r_nat = jnp.concatenate(r_parts, axis=0)
    r_exp = _dot01_right(r_nat, expand)
    s_t = jnp.dot(kb[...], wq, preferred_element_type=F32) * scale
    s_t = s_t + (c_row + r_exp)
    update(s_t, vb[...])

    @pl.when(st == n_st - 1)
    def _():
        l_col = jnp.broadcast_to(l_s[0:1, :], (ht, ht)).T[:, 0:1]
        o_ref[...] = acc_s[...] / l_col


def _sample_attention(page_table, wq, kn, vn, lfn, cache_k, cache_v, cache_logf, layer, g_pages,
                      n_heads, head_dim, t_new):
    bs, n_pages = page_table.shape
    page = cache_k.shape[2]
    hd = n_heads * head_dim
    ht = n_heads * t_new
    n_st = n_pages // g_pages
    ck = cache_k.reshape(cache_k.shape[0], cache_k.shape[1], page, hd)
    cv = cache_v.reshape(cache_v.shape[0], cache_v.shape[1], page, hd)

    def page_map(g):
        def f(b, st, pt):
            return (layer, pt[b * n_pages + (n_st - 1 - st) * g_pages + g], 0, 0)
        return f

    in_specs = [pl.BlockSpec((None, hd, ht), lambda b, st, pt: (b, 0, 0)),
                pl.BlockSpec((None, t_new, hd), lambda b, st, pt: (b, 0, 0)),
                pl.BlockSpec((None, t_new, hd), lambda b, st, pt: (b, 0, 0)),
                pl.BlockSpec((None, t_new, n_heads), lambda b, st, pt: (b, 0, 0))]
    in_specs += [pl.BlockSpec((None, None, page, hd), page_map(g)) for g in range(g_pages)]
    in_specs += [pl.BlockSpec((None, None, page, hd), page_map(g)) for g in range(g_pages)]
    in_specs += [pl.BlockSpec((None, None, page, n_heads), page_map(g)) for g in range(g_pages)]
    grid_spec = pltpu.PrefetchScalarGridSpec(
        num_scalar_prefetch=1, grid=(bs, n_st), in_specs=in_specs,
        out_specs=pl.BlockSpec((None, ht, head_dim), lambda b, st, pt: (b, 0, 0)),
        scratch_shapes=[pltpu.VMEM((g_pages * page, hd), BF16), pltpu.VMEM((g_pages * page, hd), BF16),
                        pltpu.VMEM((g_pages * page, LANES), F32), pltpu.VMEM((8, ht), F32),
                        pltpu.VMEM((8, ht), F32), pltpu.VMEM((8, ht), F32),
                        pltpu.VMEM((ht, head_dim), F32), pltpu.VMEM((8, LANES), F32)])
    return pl.pallas_call(
        functools.partial(_sattn_kernel, g_pages=g_pages, page=page, n_heads=n_heads, t_new=t_new,
                          scale=head_dim ** -0.5),
        grid_spec=grid_spec,
        out_shape=jax.ShapeDtypeStruct((bs, ht, head_dim), F32),
        compiler_params=_params("arbitrary", "arbitrary"),
        name="sample_attention",
    )(page_table.reshape(-1), wq, kn, vn, lfn, *([ck] * g_pages), *([cv] * g_pages),
      *([cache_logf] * g_pages))


def _conv_rows(window, w_ref, bdw, gcn, bcn, width):
    acc = None
    for j in range(width):
        term = w_ref[j:j + 1, :] * window(j)
        acc = term if acc is None else acc + term
    y = acc + bdw
    mu = jnp.mean(y, axis=-1, keepdims=True)
    yc = y - mu
    var = jnp.mean(yc * yc, axis=-1, keepdims=True)
    z = yc * lax.rsqrt(var + LN_EPS) * gcn + bcn
    return z * jax.nn.sigmoid(z)


def _pconv_kernel(u_ref, halo_ref, w_ref, bdw_ref, gcn_ref, bcn_ref, o_ref, ext, *, tt, halo, width, chunk):
    ti = pl.program_id(1)
    n_ext = halo + tt
    ext[0, 0:halo, :] = jnp.where(ti > 0, halo_ref[...], 0.0)
    ext[0, halo:n_ext, :] = u_ref[...]
    for r in range(1, SUBLANES):
        ext[r, 0:n_ext - SUBLANES, :] = ext[0, r:r + n_ext - SUBLANES, :]
    bdw, gcn, bcn = bdw_ref[...], gcn_ref[...], bcn_ref[...]
    first = halo - (width - 1)

    def body(c, _):
        r0 = pl.multiple_of(c * chunk, chunk)

        def window(j):
            off = first + j
            return ext[off % SUBLANES, pl.ds(r0 + (off // SUBLANES) * SUBLANES, chunk), :]

        z = _conv_rows(window, w_ref, bdw, gcn, bcn, width)
        o_ref[pl.ds(r0, chunk), :] = z.astype(o_ref.dtype)
        return 0

    lax.fori_loop(0, tt // chunk, body, 0)


def _prompt_conv(u_p, w_dw, b_dw, g_cn, b_cn, layer, batch, seq, tt):
    m_p, ch = u_p.shape
    width = w_dw.shape[1]
    halo = 32
    assert width - 1 <= halo and tt % halo == 0
    nt = seq // tt
    vec = pl.BlockSpec((None, 1, ch), lambda b, ti: (layer, 0, 0))
    return pl.pallas_call(
        functools.partial(_pconv_kernel, tt=tt, halo=halo, width=width, chunk=16),
        grid=(batch, nt),
        in_specs=[pl.BlockSpec((tt, ch), lambda b, ti: (b * nt + ti, 0)),
                  pl.BlockSpec((halo, ch), lambda b, ti: (jnp.maximum((b * nt + ti) * (tt // halo) - 1, 0), 0)),
                  pl.BlockSpec((None, width, ch), lambda b, ti: (layer, 0, 0)),
                  vec, vec, vec],
        out_specs=pl.BlockSpec((tt, ch), lambda b, ti: (b * nt + ti, 0)),
        out_shape=jax.ShapeDtypeStruct((m_p, ch), BF16),
        scratch_shapes=[pltpu.VMEM((SUBLANES, halo + tt, ch), F32)],
        compiler_params=_params("arbitrary", "arbitrary"),
        name="prompt_conv",
    )(u_p, u_p, w_dw, b_dw, g_cn, b_cn)


def _sconv_kernel(ext_ref, w_ref, bdw_ref, gcn_ref, bcn_ref, o_ref, *, n_seq, t_new, width):
    bdw, gcn, bcn = bdw_ref[...], gcn_ref[...], bcn_ref[...]
    for b in range(n_seq):
        z = _conv_rows(lambda j, b=b: ext_ref[b, j:j + t_new, :], w_ref, bdw, gcn, bcn, width)
        o_ref[b * t_new:(b + 1) * t_new, :] = z


def _sample_conv(ext_s, w_dw, b_dw, g_cn, b_cn, layer, t_new):
    n_seq, rows, ch = ext_s.shape
    width = w_dw.shape[1]
    vec = pl.BlockSpec((None, 1, ch), lambda i: (layer, 0, 0))
    return pl.pallas_call(
        functools.partial(_sconv_kernel, n_seq=n_seq, t_new=t_new, width=width),
        grid=(1,),
        in_specs=[pl.BlockSpec((n_seq, rows, ch), lambda i: (0, 0, 0)),
                  pl.BlockSpec((None, width, ch), lambda i: (layer, 0, 0)),
                  vec, vec, vec],
        out_specs=pl.BlockSpec((n_seq * t_new, ch), lambda i: (0, 0)),
        out_shape=jax.ShapeDtypeStruct((n_seq * t_new, ch), F32),
        compiler_params=_params("arbitrary"),
        name="sample_conv",
    )(ext_s, w_dw, b_dw, g_cn, b_cn)


def _largest_divisor(n, cap, multiple):
    best = None
    for t in range(multiple, min(n, cap) + 1, multiple):
        if n % t == 0:
            best = t
    assert best is not None, (n, cap, multiple)
    return best


def kernel(x_prompt, x_sample, c_prompt, c_sample, cache_k, cache_v, cache_logf, state_conv, page_table,
           w_ada, b_ada, w_ffn1_in, w_ffn1_down, w_in, b_f, w_dw, b_dw, g_cn, b_cn, w_out,
           w_ffn2_in, w_ffn2_down, ln_g, ln_b):
    batch, seq, d = x_prompt.shape
    bs, t_new, _ = x_sample.shape
    depth = w_ada.shape[0]
    n_heads, head_dim = cache_k.shape[3], cache_k.shape[4]
    aw = n_heads * head_dim
    ch = d - aw
    d_ff = w_ffn1_down.shape[1]
    width = w_dw.shape[1]
    alpha = (2 * depth) ** 0.25
    m_p, m_s = batch * seq, bs * t_new

    tm = _largest_divisor(m_p, 1024, 16)
    tm_down = _largest_divisor(m_p, 512, 16)
    tr = _largest_divisor(seq, 256, 16)
    tn_ff = _largest_divisor(d_ff, 256, LANES)
    tn_d = _largest_divisor(d, 256, LANES)
    tn_aw = _largest_divisor(aw, 512, LANES)
    tn_ch = _largest_divisor(ch, 256, LANES)
    tn_ada = _largest_divisor(N_MOD * d, 512, LANES)
    tq = _largest_divisor(seq, 512, LANES)
    tt = _largest_divisor(seq, 256, 32)
    g_pages = _largest_divisor(page_table.shape[1], 4, 1)

    n_seq_all = batch + bs
    c_all = jnp.concatenate([c_prompt, c_sample, jnp.zeros((-n_seq_all % 8, d), F32)], axis=0)
    b_ada3 = b_ada.reshape(depth, 1, N_MOD * d)
    mods = []
    for l in range(depth):
        m = _adaln(c_all, w_ada, b_ada3, l, tn_ada)
        mod_p = m[:batch].reshape(batch, N_MOD, 1, d)
        mod_s = jnp.repeat(m[batch:n_seq_all].reshape(bs, N_MOD, d).transpose(1, 0, 2), t_new, axis=1)
        mods.append((mod_p, mod_s))

    c_f = 3 * aw
    c_a = c_f + n_heads
    c_gl = c_a + ch
    w_f = jnp.pad(w_in[:, :, c_f:c_a], ((0, 0), (0, 0), (0, LANES - n_heads)))
    b_f3 = jnp.pad(b_f, ((0, 0), (0, LANES - n_heads))).reshape(depth, 1, LANES)
    w_a = w_in[:, :, c_a:c_gl]
    w_gl = w_in[:, :, c_gl:]
    ln_g4 = ln_g.reshape(depth, 3, 1, d)
    ln_b4 = ln_b.reshape(depth, 3, 1, d)
    b_dw3, g_cn3, b_cn3 = (v.reshape(depth, 1, ch) for v in (b_dw, g_cn, b_cn))
    eye_h = jnp.eye(n_heads, dtype=bool)

    x = (x_prompt.reshape(m_p, d), x_sample.reshape(m_s, d))
    xm = _modulate(x[0], x[1], mods[0][0], mods[0][1], 0, tr, seq)

    def ffn(x, xm, w_in_l, w_down_l, l, which_gate, which_ln, next_mod):
        (h,) = _mm([xm[0]], [xm[1]],
                   [[(w_in_l, 0, 0)], [(w_in_l, 0, d_ff // tn_ff)]],
                   tn_ff, d_ff, _epi_swiglu, [BF16], tm, l, name="ffn_in")
        (y,) = _mm([h[0]], [h[1]], [[(w_down_l, 0, 0)]], tn_d, d, _epi_plain, [F32], tm_down, l,
                   name="ffn_down")
        return _ln_mod(x, y, mods[l][0], mods[l][1], which_gate, 0.5, alpha, ln_g4, ln_b4, l, which_ln,
                       next_mod, tr, seq)

    outs = {k: [] for k in ("kp", "vp", "lfp", "cvp", "ks", "vs", "lfs", "cvs")}
    for l in range(depth):
        mod_p, mod_s = mods[l]
        x, xm = ffn(x, xm, w_ffn1_in, w_ffn1_down, l, 2, 0, (mod_p, mod_s, 3))

        (q,) = _mm([xm[0]], [xm[1]], [[(w_in, 0, 0)]], tn_aw, aw, _epi_plain, [BF16], tm, l, name="proj_q")
        (k,) = _mm([xm[0]], [xm[1]], [[(w_in, 0, aw // tn_aw)]], tn_aw, aw, _epi_plain, [F32], tm, l,
                   name="proj_k")
        (v,) = _mm([xm[0]], [xm[1]], [[(w_in, 0, 2 * aw // tn_aw)]], tn_aw, aw, _epi_plain, [F32], tm, l,
                   name="proj_v")
        (lf,) = _mm([xm[0]], [xm[1]], [[(w_f, 0, 0)]], LANES, LANES, _epi_logsig, [F32], tm, l, bias=b_f3,
                    name="proj_f")
        (u,) = _mm([xm[0]], [xm[1]], [[(w_a, 0, 0)], [(w_gl, 0, 0)]], tn_ch, ch, _epi_glu, [F32], tm, l,
                   name="proj_glu")

        attn_p = _prompt_attention(q[0], k[0], v[0], lf[0], batch, seq, n_heads, head_dim, tq)
        conv_p = _prompt_conv(u[0], w_dw, b_dw3, g_cn3, b_cn3, l, batch, seq, tt)

        q_s = q[1].reshape(bs, t_new, n_heads, head_dim).transpose(0, 2, 3, 1)
        wq = jnp.where(eye_h[None, :, None, :, None], q_s[:, :, :, None, :], 0).reshape(
            bs, aw, n_heads * t_new)
        lfn = lf[1][:, :n_heads].reshape(bs, t_new, n_heads)
        o_s = _sample_attention(page_table, wq, k[1].reshape(bs, t_new, aw), v[1].reshape(bs, t_new, aw),
                                lfn, cache_k, cache_v, cache_logf, l, g_pages, n_heads, head_dim, t_new)
        attn_s = o_s.reshape(bs, n_heads, t_new, head_dim).transpose(0, 2, 1, 3).reshape(m_s, aw).astype(BF16)
        u_s = u[1].reshape(bs, t_new, ch)
        ext_s = jnp.concatenate([state_conv[l], u_s, jnp.zeros((bs, -(width - 1 + t_new) % 8, ch), F32)],
                                axis=1)
        conv_s = _sample_conv(ext_s, w_dw, b_dw3, g_cn3, b_cn3, l, t_new).astype(BF16)

        (y,) = _mm([attn_p, conv_p], [attn_s, conv_s], [[(w_out, 0, 0), (w_out, 1, 0)]],
                   tn_d, d, _epi_plain, [F32], tm, l, name="proj_out")
        x, xm = _ln_mod(x, y, mod_p, mod_s, 5, 1.0, alpha, ln_g4, ln_b4, l, 1, (mod_p, mod_s, 6), tr, seq)

        next_mod = (mods[l + 1][0], mods[l + 1][1], 0) if l + 1 < depth else None
        x, xm = ffn(x, xm, w_ffn2_in, w_ffn2_down, l, 8, 2, next_mod)

        outs["kp"].append(k[0].reshape(batch, seq, n_heads, head_dim))
        outs["vp"].append(v[0].reshape(batch, seq, n_heads, head_dim))
        outs["lfp"].append(lf[0][:, :n_heads].reshape(batch, seq, n_heads))
        outs["cvp"].append(u[0].reshape(batch, seq, ch)[:, seq - (width - 1):])
        outs["ks"].append(k[1].reshape(bs, t_new, n_heads, head_dim))
        outs["vs"].append(v[1].reshape(bs, t_new, n_heads, head_dim))
        outs["lfs"].append(lfn)
        outs["cvs"].append(jnp.concatenate([state_conv[l], u_s], axis=1)[:, -(width - 1):])

    return (x[0].reshape(batch, seq, d), x[1].reshape(bs, t_new, d),
            jnp.stack(outs["kp"]), jnp.stack(outs["vp"]), jnp.stack(outs["lfp"]), jnp.stack(outs["cvp"]),
            jnp.stack(outs["ks"]), jnp.stack(outs["vs"]), jnp.stack(outs["lfs"]), jnp.stack(outs["cvs"]))
```

```python
import functools

import jax
import jax.numpy as jnp
from jax import lax
from jax.experimental import pallas as pl
from jax.experimental.pallas import tpu as pltpu

F32 = jnp.float32
BF16 = jnp.bfloat16
LN_EPS = 1e-5
N_MOD = 9
LANES = 128
SUBLANES = 8
VMEM_LIMIT_BYTES = 60 * 1024 * 1024


def _params(*sem):
    return pltpu.CompilerParams(dimension_semantics=sem, vmem_limit_bytes=VMEM_LIMIT_BYTES)


def _split3(x):
    hi = x.astype(BF16)
    r1 = x - hi.astype(F32)
    mid = r1.astype(BF16)
    lo = (r1 - mid.astype(F32)).astype(BF16)
    return hi, mid, lo


def _dot01_left(m01, x):
    return sum(jnp.dot(m01, t, preferred_element_type=F32) for t in _split3(x))


def _dot01_right(x, m01):
    return sum(jnp.dot(t, m01, preferred_element_type=F32) for t in _split3(x))


def _adaln_kernel(c_ref, w_ref, b_ref, o_ref):
    c = c_ref[...]
    x = (c * jax.nn.sigmoid(c)).astype(BF16)
    o_ref[...] = jnp.dot(x, w_ref[...].astype(BF16), preferred_element_type=F32) + b_ref[...]


def _adaln(c_all, w_ada, b_ada3, layer, tn):
    rows, d = c_all.shape
    n = w_ada.shape[2]
    return pl.pallas_call(
        _adaln_kernel,
        grid=(n // tn,),
        in_specs=[pl.BlockSpec((rows, d), lambda j: (0, 0)),
                  pl.BlockSpec((None, d, tn), lambda j: (layer, 0, j)),
                  pl.BlockSpec((None, 1, tn), lambda j: (layer, 0, j))],
        out_specs=pl.BlockSpec((rows, tn), lambda j: (0, j)),
        out_shape=jax.ShapeDtypeStruct((rows, n), F32),
        compiler_params=_params("arbitrary"),
        name="adaln",
    )(c_all, w_ada, b_ada3)


def _epi_plain(accs, bias):
    return [jnp.concatenate(accs, axis=1) if len(accs) > 1 else accs[0]]


def _epi_swiglu(accs, bias):
    half = len(accs) // 2
    g = jnp.concatenate(accs[:half], axis=1)
    u = jnp.concatenate(accs[half:], axis=1)
    return [g * jax.nn.sigmoid(g) * u]


def _epi_glu(accs, bias):
    half = len(accs) // 2
    a = jnp.concatenate(accs[:half], axis=1)
    gl = jnp.concatenate(accs[half:], axis=1)
    return [a * jax.nn.sigmoid(gl)]


def _epi_logsig(accs, bias):
    z = accs[0] + bias
    return [jnp.minimum(z, 0.0) - jnp.log1p(jnp.exp(-jnp.abs(z)))]


def _mm_kernel(*refs, n_x, n_br, n_out, n_p, has_bias, has_add, epilogue):
    it = iter(refs)
    x_p = [next(it) for _ in range(n_x)]
    x_s = [next(it) for _ in range(n_x)]
    w = [[next(it) for _ in range(n_x)] for _ in range(n_br)]
    bias = next(it) if has_bias else None
    add_p = next(it) if has_add else None
    add_s = next(it) if has_add else None
    o_p = [next(it) for _ in range(n_out)]
    o_s = [next(it) for _ in range(n_out)]
    wb = [[next(it) for _ in range(n_x)] for _ in range(n_br)]
    i = pl.program_id(1)

    @pl.when(i == 0)
    def _():
        for b in range(n_br):
            for p in range(n_x):
                wb[b][p][...] = w[b][p][...].astype(BF16)

    def run(xs, add, outs):
        accs = []
        for b in range(n_br):
            acc = None
            for p in range(n_x):
                d = jnp.dot(xs[p][...], wb[b][p][...], preferred_element_type=F32)
                acc = d if acc is None else acc + d
            accs.append(acc)
        if has_add:
            accs[0] = accs[0] + add[...]
        vals = epilogue(accs, bias[...] if has_bias else None)
        for o, v in zip(outs, vals):
            o[...] = v.astype(o.dtype)

    @pl.when(i < n_p)
    def _():
        run(x_p, add_p, o_p)

    @pl.when(i == n_p)
    def _():
        run(x_s, add_s, o_s)


def _mm(xs, w_specs, w_tn, out_tn, n_cols, epilogue, out_dtypes, tm, layer, bias=None, add=None, name="mm"):
    n_x, n_br, n_out = len(xs), len(w_specs), len(out_dtypes)
    m_p, m_s = xs[0][0][0].shape[0], xs[0][0][1].shape[0]
    n_p = m_p // tm
    grid = (pl.cdiv(n_cols, out_tn), n_p + 1)
    in_specs, args = [], []
    for (x_p, _), kp, kb in xs:
        in_specs.append(pl.BlockSpec((tm, kp), lambda j, i, kb=kb: (jnp.minimum(i, n_p - 1), kb)))
        args.append(x_p)
    for (_, x_s), kp, kb in xs:
        in_specs.append(pl.BlockSpec((m_s, kp), lambda j, i, kb=kb: (0, kb)))
        args.append(x_s)
    scratch = []
    for b in range(n_br):
        for p in range(n_x):
            w, rb, cm, co, cmax = w_specs[b][p]
            kp = xs[p][1]
            in_specs.append(pl.BlockSpec(
                (None, kp, w_tn),
                lambda j, i, rb=rb, cm=cm, co=co, cmax=cmax: (layer, rb, jnp.minimum(cm * j + co, cmax))))
            args.append(w)
            scratch.append(pltpu.VMEM((kp, w_tn), BF16))
    if bias is not None:
        in_specs.append(pl.BlockSpec((None, 1, out_tn), lambda j, i: (layer, 0, j)))
        args.append(bias)
    out_p = pl.BlockSpec((tm, out_tn), lambda j, i: (jnp.minimum(i, n_p - 1), j))
    out_s = pl.BlockSpec((m_s, out_tn), lambda j, i: (0, j))
    if add is not None:
        in_specs += [out_p, out_s]
        args += [add[0], add[1]]
    out_specs = [out_p for _ in out_dtypes] + [out_s for _ in out_dtypes]
    out_shape = ([jax.ShapeDtypeStruct((m_p, n_cols), dt) for dt in out_dtypes]
                 + [jax.ShapeDtypeStruct((m_s, n_cols), dt) for dt in out_dtypes])
    outs = pl.pallas_call(
        functools.partial(_mm_kernel, n_x=n_x, n_br=n_br, n_out=n_out, n_p=n_p,
                          has_bias=bias is not None, has_add=add is not None, epilogue=epilogue),
        grid=grid, in_specs=in_specs, out_specs=out_specs, out_shape=out_shape,
        scratch_shapes=scratch,
        compiler_params=_params("arbitrary", "arbitrary"),
        name=name,
    )(*args)
    return [(outs[k], outs[n_out + k]) for k in range(n_out)]


def _w(w, row_block=0, col_off=0, col_mul=1, col_max=2 ** 30):
    return (w, row_block, col_mul, col_off, col_max)


def _mod_kernel(x_p, x_s, sh_p, sc_p, sh_s, sc_s, o_p, o_s, *, n_p):
    i = pl.program_id(0)

    @pl.when(i < n_p)
    def _():
        o_p[...] = (x_p[...] * (1.0 + sc_p[...]) + sh_p[...]).astype(o_p.dtype)

    @pl.when(i == n_p)
    def _():
        o_s[...] = (x_s[...] * (1.0 + sc_s[...]) + sh_s[...]).astype(o_s.dtype)


def _mod_specs(n_p, steps_per_seq, which, d, m_s):
    p = pl.BlockSpec((None, None, 1, d),
                     lambda i, w=which: (jnp.minimum(i, n_p - 1) // steps_per_seq, w, 0, 0))
    s = pl.BlockSpec((None, m_s, d), lambda i, w=which: (w, 0, 0))
    return p, s


def _modulate(x_p, x_s, mod_p, mod_s, which_shift, tr, seq):
    m_p, d = x_p.shape
    m_s = x_s.shape[0]
    n_p = m_p // tr
    spp = seq // tr
    sh_p, sh_s = _mod_specs(n_p, spp, which_shift, d, m_s)
    sc_p, sc_s = _mod_specs(n_p, spp, which_shift + 1, d, m_s)
    row_p = pl.BlockSpec((tr, d), lambda i: (jnp.minimum(i, n_p - 1), 0))
    row_s = pl.BlockSpec((m_s, d), lambda i: (0, 0))
    return pl.pallas_call(
        functools.partial(_mod_kernel, n_p=n_p),
        grid=(n_p + 1,),
        in_specs=[row_p, row_s, sh_p, sc_p, sh_s, sc_s],
        out_specs=[row_p, row_s],
        out_shape=[jax.ShapeDtypeStruct((m_p, d), BF16), jax.ShapeDtypeStruct((m_s, d), BF16)],
        compiler_params=_params("arbitrary"),
        name="modulate",
    )(x_p, x_s, mod_p, mod_p, mod_s, mod_s)


def _ln_kernel(*refs, n_p, alpha, coef, with_mod):
    if with_mod:
        (x_p, y_p, x_s, y_s, g_p, g_s, sh_p, sc_p, sh_s, sc_s, lng, lnb, o_p, o_s, m_p, m_s) = refs
    else:
        (x_p, y_p, x_s, y_s, g_p, g_s, lng, lnb, o_p, o_s) = refs
    i = pl.program_id(0)

    def run(x, y, gate, shift, scale, o, mo):
        z = alpha * x[...] + coef * gate[...] * y[...]
        mu = jnp.mean(z, axis=-1, keepdims=True)
        zc = z - mu
        var = jnp.mean(zc * zc, axis=-1, keepdims=True)
        out = zc * lax.rsqrt(var + LN_EPS) * lng[...] + lnb[...]
        o[...] = out
        if with_mod:
            mo[...] = (out * (1.0 + scale[...]) + shift[...]).astype(mo.dtype)

    @pl.when(i < n_p)
    def _():
        run(x_p, y_p, g_p, sh_p if with_mod else None, sc_p if with_mod else None, o_p,
            m_p if with_mod else None)

    @pl.when(i == n_p)
    def _():
        run(x_s, y_s, g_s, sh_s if with_mod else None, sc_s if with_mod else None, o_s,
            m_s if with_mod else None)


def _ln_mod(x, y, mod_p, mod_s, which_gate, coef, alpha, ln_g, ln_b, layer, which_ln, next_mod, tr, seq):
    (x_p, x_s), (y_p, y_s) = x, y
    m_p, d = x_p.shape
    m_s = x_s.shape[0]
    n_p = m_p // tr
    spp = seq // tr
    row_p = pl.BlockSpec((tr, d), lambda i: (jnp.minimum(i, n_p - 1), 0))
    row_s = pl.BlockSpec((m_s, d), lambda i: (0, 0))
    g_p, g_s = _mod_specs(n_p, spp, which_gate, d, m_s)
    ln_spec = pl.BlockSpec((None, None, 1, d), lambda i: (layer, which_ln, 0, 0))
    in_specs = [row_p, row_p, row_s, row_s, g_p, g_s]
    args = [x_p, y_p, x_s, y_s, mod_p, mod_s]
    out_specs = [row_p, row_s]
    out_shape = [jax.ShapeDtypeStruct((m_p, d), F32), jax.ShapeDtypeStruct((m_s, d), F32)]
    if next_mod is not None:
        nm_p, nm_s, ws = next_mod
        sh_p, sh_s = _mod_specs(n_p, spp, ws, d, m_s)
        sc_p, sc_s = _mod_specs(n_p, spp, ws + 1, d, m_s)
        in_specs += [sh_p, sc_p, sh_s, sc_s]
        args += [nm_p, nm_p, nm_s, nm_s]
        out_specs += [row_p, row_s]
        out_shape += [jax.ShapeDtypeStruct((m_p, d), BF16), jax.ShapeDtypeStruct((m_s, d), BF16)]
    in_specs += [ln_spec, ln_spec]
    args += [ln_g, ln_b]
    outs = pl.pallas_call(
        functools.partial(_ln_kernel, n_p=n_p, alpha=alpha, coef=coef, with_mod=next_mod is not None),
        grid=(n_p + 1,), in_specs=in_specs, out_specs=out_specs, out_shape=out_shape,
        compiler_params=_params("arbitrary"),
        name="ln_mod",
    )(*args)
    if next_mod is None:
        return (outs[0], outs[1]), None
    return (outs[0], outs[1]), (outs[2], outs[3])


def _pattn_kernel(q_ref, k_ref, v_ref, lf_ref, o_ref, kb, vb, cum, cum_t, *, tq, seq, scale):
    h = pl.program_id(1)
    qi = pl.program_id(2)
    n_blk = seq // tq

    @pl.when(qi == 0)
    def _():
        kb[...] = k_ref[...].astype(BF16)
        vb[...] = v_ref[...].astype(BF16)

    @pl.when((qi == 0) & (h == 0))
    def _():
        r = lax.broadcasted_iota(jnp.int32, (LANES, LANES), 0)
        c = lax.broadcasted_iota(jnp.int32, (LANES, LANES), 1)
        tri = (r >= c).astype(BF16)
        carry = jnp.zeros((1, LANES), F32)
        for blk in range(seq // LANES):
            x = lf_ref[blk * LANES:(blk + 1) * LANES, :]
            cb = _dot01_left(tri, x) + carry
            cum[blk * LANES:(blk + 1) * LANES, :] = cb
            carry = cb[LANES - 1:LANES, :]
        for blk in range(n_blk):
            cum_t[blk] = cum[blk * tq:(blk + 1) * tq, :].T

    q = q_ref[...]
    lane = lax.broadcasted_iota(jnp.int32, (tq, LANES), 1)
    cq = jnp.sum(jnp.where(lane == h, cum[pl.ds(pl.multiple_of(qi * tq, tq), tq), :], 0.0),
                 axis=1, keepdims=True)

    def block(ki, carry, masked):
        m, l, acc = carry
        start = pl.multiple_of(ki * tq, tq)
        s = lax.dot_general(q, kb[pl.ds(start, tq), :], (((1,), (1,)), ((), ())),
                            preferred_element_type=F32) * scale
        ck = cum_t[ki, pl.ds(h, 1), :]
        s = s + (cq - ck)
        if masked:
            row = lax.broadcasted_iota(jnp.int32, (tq, tq), 0)
            col = lax.broadcasted_iota(jnp.int32, (tq, tq), 1)
            s = jnp.where(col <= row, s, -jnp.inf)
        m_new = jnp.maximum(m, jnp.max(s, axis=1, keepdims=True))
        a = jnp.exp(m - m_new)
        p = jnp.exp(s - m_new)
        l = a * l + jnp.sum(p, axis=1, keepdims=True)
        acc = a * acc + jnp.dot(p.astype(BF16), vb[pl.ds(start, tq), :], preferred_element_type=F32)
        return m_new, l, acc

    init = (jnp.full((tq, 1), -jnp.inf, F32), jnp.zeros((tq, 1), F32), jnp.zeros((tq, q_ref.shape[1]), F32))
    carry = lax.fori_loop(0, qi, lambda ki, cr: block(ki, cr, False), init)
    m, l, acc = block(qi, carry, True)
    o_ref[...] = (acc / l).astype(o_ref.dtype)


def _prompt_attention(q_p, k_p, v_p, lf_p, batch, seq, n_heads, head_dim, tq):
    m_p = q_p.shape[0]
    nq = seq // tq
    return pl.pallas_call(
        functools.partial(_pattn_kernel, tq=tq, seq=seq, scale=head_dim ** -0.5),
        grid=(batch, n_heads, nq),
        in_specs=[pl.BlockSpec((tq, head_dim), lambda b, h, qi: (b * nq + qi, h)),
                  pl.BlockSpec((seq, head_dim), lambda b, h, qi: (b, h)),
                  pl.BlockSpec((seq, head_dim), lambda b, h, qi: (b, h)),
                  pl.BlockSpec((seq, LANES), lambda b, h, qi: (b, 0))],
        out_specs=pl.BlockSpec((tq, head_dim), lambda b, h, qi: (b * nq + qi, h)),
        out_shape=jax.ShapeDtypeStruct((m_p, n_heads * head_dim), BF16),
        scratch_shapes=[pltpu.VMEM((seq, head_dim), BF16), pltpu.VMEM((seq, head_dim), BF16),
                        pltpu.VMEM((seq, LANES), F32), pltpu.VMEM((nq, LANES, tq), F32)],
        compiler_params=_params("arbitrary", "arbitrary", "arbitrary"),
        name="prompt_attention",
    )(q_p, k_p, v_p, lf_p)


def _sattn_kernel(pt_ref, wq_ref, kn_ref, vn_ref, lfn_ref, *refs, g_pages, page, n_heads, t_new, scale):
    k_refs = refs[:g_pages]
    v_refs = refs[g_pages:2 * g_pages]
    lf_refs = refs[2 * g_pages:3 * g_pages]
    o_ref = refs[3 * g_pages]
    kb, vb, lfbuf, crow_s, m_s, l_s, acc_s, carry_s = refs[3 * g_pages + 1:]
    st = pl.program_id(1)
    n_st = pl.num_programs(1)
    ht = n_heads * t_new
    hd = kb.shape[1]
    d = hd // n_heads
    wq = wq_ref[...]

    expand = (lax.broadcasted_iota(jnp.int32, (LANES, ht), 0)
              == lax.broadcasted_iota(jnp.int32, (LANES, ht), 1) // t_new).astype(BF16)
    key_l = lax.broadcasted_iota(jnp.int32, (page, LANES), 0)
    key_q = lax.broadcasted_iota(jnp.int32, (page, ht), 0)
    lane_t = lax.broadcasted_iota(jnp.int32, (page, ht), 1) % t_new

    def update(s_t, v_chunk):
        m_old = m_s[0:1, :]
        m_new = jnp.maximum(m_old, jnp.max(s_t, axis=0, keepdims=True))
        a = jnp.exp(m_old - m_new)
        p = jnp.exp(s_t - m_new)
        l_s[...] = jnp.broadcast_to(a * l_s[0:1, :] + jnp.sum(p, axis=0, keepdims=True), l_s.shape)
        m_s[...] = jnp.broadcast_to(m_new, m_s.shape)
        o_full = jnp.dot(p.T.astype(BF16), v_chunk, preferred_element_type=F32)
        o_diag = jnp.concatenate(
            [o_full[hh * t_new:(hh + 1) * t_new, hh * d:(hh + 1) * d] for hh in range(n_heads)], axis=0)
        a_col = jnp.broadcast_to(a, (ht, ht)).T[:, 0:1]
        acc_s[...] = a_col * acc_s[...] + o_diag

    @pl.when(st == 0)
    def _():
        lfbuf[...] = jnp.zeros(lfbuf.shape, F32)
        carry_s[...] = jnp.zeros(carry_s.shape, F32)
        m_s[...] = jnp.full(m_s.shape, -jnp.inf, F32)
        l_s[...] = jnp.zeros(l_s.shape, F32)
        acc_s[...] = jnp.zeros(acc_s.shape, F32)
        pad = jnp.zeros((page - t_new, hd), F32)
        kn = jnp.concatenate([kn_ref[...], pad], axis=0).astype(BF16)
        vn = jnp.concatenate([vn_ref[...], pad], axis=0).astype(BF16)
        lfbuf[0:t_new, 0:n_heads] = lfn_ref[...]
        tri = (lax.broadcasted_iota(jnp.int32, (page, page), 0)
               >= lax.broadcasted_iota(jnp.int32, (page, page), 1)).astype(BF16)
        c_nat = _dot01_left(tri, lfbuf[0:page, :])
        c_exp = _dot01_right(c_nat, expand)
        c_row = jnp.sum(jnp.where(key_q == lane_t, c_exp, 0.0), axis=0, keepdims=True)
        crow_s[...] = jnp.broadcast_to(c_row, crow_s.shape)
        s_t = jnp.dot(kn, wq, preferred_element_type=F32) * scale
        s_t = s_t + (c_row - c_exp)
        s_t = jnp.where(key_q <= lane_t, s_t, -jnp.inf)
        update(s_t, vn)

    c_row = crow_s[0:1, :]
    for g in range(g_pages):
        for hh in range(n_heads):
            kb[g * page:(g + 1) * page, hh * d:(hh + 1) * d] = (
                k_refs[g][pl.ds(hh, page, stride=n_heads), :].astype(BF16))
            vb[g * page:(g + 1) * page, hh * d:(hh + 1) * d] = (
                v_refs[g][pl.ds(hh, page, stride=n_heads), :].astype(BF16))
        lfbuf[g * page:(g + 1) * page, 0:n_heads] = lf_refs[g][...].astype(F32)
    carry = carry_s[0:1, :]
    r_parts = [None] * g_pages
    for g in reversed(range(g_pages)):
        x = lfbuf[g * page:(g + 1) * page, :]
        y = x
        shift = 1
        while shift < page:
            y = y + jnp.where(key_l + shift < page, pltpu.roll(y, page - shift, axis=0), 0.0)
            shift *= 2
        excl = jnp.where(key_l + 1 < page, pltpu.roll(y, page - 1, axis=0), 0.0)
        r_parts[g] = excl + carry
        carry = carry + y[0:1, :]
    carry_s[...] = jnp.broadcast_to(carry, carry_s.shape)
    r_nat = jnp.concatenate(r_parts, axis=0)
    r_exp = _dot01_right(r_nat, expand)
    s_t = jnp.dot(kb[...], wq, preferred_element_type=F32) * scale
    s_t = s_t + (c_row + r_exp)
    update(s_t, vb[...])

    @pl.when(st == n_st - 1)
    def _():
        l_col = jnp.broadcast_to(l_s[0:1, :], (ht, ht)).T[:, 0:1]
        o_ref[...] = acc_s[...] / l_col


def _sample_attention(page_table, wq, kn, vn, lfn, cache_k, cache_v, cache_logf, layer, g_pages,
                      n_heads, head_dim, t_new):
    bs, n_pages = page_table.shape
    page = cache_k.shape[2]
    hd = n_heads * head_dim
    ht = n_heads * t_new
    n_st = n_pages // g_pages
    ck = cache_k.reshape(cache_k.shape[0], cache_k.shape[1], page * n_heads, head_dim)
    cv = cache_v.reshape(cache_v.shape[0], cache_v.shape[1], page * n_heads, head_dim)

    def page_map(g):
        def f(b, st, pt):
            return (layer, pt[b * n_pages + (n_st - 1 - st) * g_pages + g], 0, 0)
        return f

    in_specs = [pl.BlockSpec((None, hd, ht), lambda b, st, pt: (b, 0, 0)),
                pl.BlockSpec((None, t_new, hd), lambda b, st, pt: (b, 0, 0)),
                pl.BlockSpec((None, t_new, hd), lambda b, st, pt: (b, 0, 0)),
                pl.BlockSpec((None, t_new, n_heads), lambda b, st, pt: (b, 0, 0))]
    in_specs += [pl.BlockSpec((None, None, page * n_heads, head_dim), page_map(g)) for g in range(g_pages)]
    in_specs += [pl.BlockSpec((None, None, page * n_heads, head_dim), page_map(g)) for g in range(g_pages)]
    in_specs += [pl.BlockSpec((None, None, page, n_heads), page_map(g)) for g in range(g_pages)]
    grid_spec = pltpu.PrefetchScalarGridSpec(
        num_scalar_prefetch=1, grid=(bs, n_st), in_specs=in_specs,
        out_specs=pl.BlockSpec((None, ht, head_dim), lambda b, st, pt: (b, 0, 0)),
        scratch_shapes=[pltpu.VMEM((g_pages * page, hd), BF16), pltpu.VMEM((g_pages * page, hd), BF16),
                        pltpu.VMEM((g_pages * page, LANES), F32), pltpu.VMEM((8, ht), F32),
                        pltpu.VMEM((8, ht), F32), pltpu.VMEM((8, ht), F32),
                        pltpu.VMEM((ht, head_dim), F32), pltpu.VMEM((8, LANES), F32)])
    return pl.pallas_call(
        functools.partial(_sattn_kernel, g_pages=g_pages, page=page, n_heads=n_heads, t_new=t_new,
                          scale=head_dim ** -0.5),
        grid_spec=grid_spec,
        out_shape=jax.ShapeDtypeStruct((bs, ht, head_dim), F32),
        compiler_params=_params("arbitrary", "arbitrary"),
        name="sample_attention",
    )(page_table.reshape(-1), wq, kn, vn, lfn, *([ck] * g_pages), *([cv] * g_pages),
      *([cache_logf] * g_pages))


def _conv_rows(window, w_ref, bdw, gcn, bcn, width):
    acc = None
    for j in range(width):
        term = w_ref[j:j + 1, :] * window(j)
        acc = term if acc is None else acc + term
    y = acc + bdw
    mu = jnp.mean(y, axis=-1, keepdims=True)
    yc = y - mu
    var = jnp.mean(yc * yc, axis=-1, keepdims=True)
    z = yc * lax.rsqrt(var + LN_EPS) * gcn + bcn
    return z * jax.nn.sigmoid(z)


def _pconv_kernel(u_ref, halo_ref, w_ref, bdw_ref, gcn_ref, bcn_ref, o_ref, ext, *, tt, halo, width, chunk):
    ti = pl.program_id(1)
    n_ext = halo + tt
    ext[0, 0:halo, :] = jnp.where(ti > 0, halo_ref[...], 0.0)
    ext[0, halo:n_ext, :] = u_ref[...]
    for r in range(1, SUBLANES):
        ext[r, 0:n_ext - SUBLANES, :] = ext[0, r:r + n_ext - SUBLANES, :]
    bdw, gcn, bcn = bdw_ref[...], gcn_ref[...], bcn_ref[...]
    first = halo - (width - 1)

    def body(c, _):
        r0 = pl.multiple_of(c * chunk, chunk)

        def window(j):
            off = first + j
            return ext[off % SUBLANES, pl.ds(r0 + (off // SUBLANES) * SUBLANES, chunk), :]

        z = _conv_rows(window, w_ref, bdw, gcn, bcn, width)
        o_ref[pl.ds(r0, chunk), :] = z.astype(o_ref.dtype)
        return 0

    lax.fori_loop(0, tt // chunk, body, 0)


def _prompt_conv(u_p, w_dw, b_dw, g_cn, b_cn, layer, batch, seq, tt):
    m_p, ch = u_p.shape
    width = w_dw.shape[1]
    halo = 32
    assert width - 1 <= halo and tt % halo == 0
    nt = seq // tt
    vec = pl.BlockSpec((None, 1, ch), lambda b, ti: (layer, 0, 0))
    return pl.pallas_call(
        functools.partial(_pconv_kernel, tt=tt, halo=halo, width=width, chunk=16),
        grid=(batch, nt),
        in_specs=[pl.BlockSpec((tt, ch), lambda b, ti: (b * nt + ti, 0)),
                  pl.BlockSpec((halo, ch), lambda b, ti: (jnp.maximum((b * nt + ti) * (tt // halo) - 1, 0), 0)),
                  pl.BlockSpec((None, width, ch), lambda b, ti: (layer, 0, 0)),
                  vec, vec, vec],
        out_specs=pl.BlockSpec((tt, ch), lambda b, ti: (b * nt + ti, 0)),
        out_shape=jax.ShapeDtypeStruct((m_p, ch), BF16),
        scratch_shapes=[pltpu.VMEM((SUBLANES, halo + tt, ch), F32)],
        compiler_params=_params("arbitrary", "arbitrary"),
        name="prompt_conv",
    )(u_p, u_p, w_dw, b_dw, g_cn, b_cn)


def _sconv_kernel(ext_ref, w_ref, bdw_ref, gcn_ref, bcn_ref, o_ref, *, n_seq, t_new, width):
    bdw, gcn, bcn = bdw_ref[...], gcn_ref[...], bcn_ref[...]
    for b in range(n_seq):
        z = _conv_rows(lambda j, b=b: ext_ref[b, j:j + t_new, :], w_ref, bdw, gcn, bcn, width)
        o_ref[b * t_new:(b + 1) * t_new, :] = z


def _sample_conv(ext_s, w_dw, b_dw, g_cn, b_cn, layer, t_new):
    n_seq, rows, ch = ext_s.shape
    width = w_dw.shape[1]
    vec = pl.BlockSpec((None, 1, ch), lambda i: (layer, 0, 0))
    return pl.pallas_call(
        functools.partial(_sconv_kernel, n_seq=n_seq, t_new=t_new, width=width),
        grid=(1,),
        in_specs=[pl.BlockSpec((n_seq, rows, ch), lambda i: (0, 0, 0)),
                  pl.BlockSpec((None, width, ch), lambda i: (layer, 0, 0)),
                  vec, vec, vec],
        out_specs=pl.BlockSpec((n_seq * t_new, ch), lambda i: (0, 0)),
        out_shape=jax.ShapeDtypeStruct((n_seq * t_new, ch), F32),
        compiler_params=_params("arbitrary"),
        name="sample_conv",
    )(ext_s, w_dw, b_dw, g_cn, b_cn)


def _largest_divisor(n, cap, multiple):
    best = None
    for t in range(multiple, min(n, cap) + 1, multiple):
        if n % t == 0:
            best = t
    assert best is not None, (n, cap, multiple)
    return best


def kernel(x_prompt, x_sample, c_prompt, c_sample, cache_k, cache_v, cache_logf, state_conv, page_table,
           w_ada, b_ada, w_ffn1_in, w_ffn1_down, w_in, b_f, w_dw, b_dw, g_cn, b_cn, w_out,
           w_ffn2_in, w_ffn2_down, ln_g, ln_b):
    batch, seq, d = x_prompt.shape
    bs, t_new, _ = x_sample.shape
    depth = w_ada.shape[0]
    n_heads, head_dim = cache_k.shape[3], cache_k.shape[4]
    aw = n_heads * head_dim
    ch = d - aw
    d_ff = w_ffn1_down.shape[1]
    width = w_dw.shape[1]
    alpha = (2 * depth) ** 0.25
    m_p, m_s = batch * seq, bs * t_new

    tm = _largest_divisor(m_p, 512, 16)
    tr = _largest_divisor(seq, 256, 16)
    tn_ff = _largest_divisor(d_ff, 256, LANES)
    tn_aw = _largest_divisor(aw, 256, LANES)
    tn_ch = _largest_divisor(ch, 256, LANES)
    tn_d = _largest_divisor(d, 512, LANES)
    nb_ff, nb_aw, nb_ch = d_ff // tn_ff, aw // tn_aw, ch // tn_ch
    k_split = 2 if d_ff % (2 * LANES) == 0 else 1
    tn_ada = _largest_divisor(N_MOD * d, 512, LANES)
    tq = _largest_divisor(seq, 512, LANES)
    tt = _largest_divisor(seq, 256, 32)
    g_pages = _largest_divisor(page_table.shape[1], 4, 1)

    n_seq_all = batch + bs
    c_all = jnp.concatenate([c_prompt, c_sample, jnp.zeros((-n_seq_all % 8, d), F32)], axis=0)
    b_ada3 = b_ada.reshape(depth, 1, N_MOD * d)
    mods = []
    for l in range(depth):
        m = _adaln(c_all, w_ada, b_ada3, l, tn_ada)
        mod_p = m[:batch].reshape(batch, N_MOD, 1, d)
        mod_s = jnp.repeat(m[batch:n_seq_all].reshape(bs, N_MOD, d).transpose(1, 0, 2), t_new, axis=1)
        mods.append((mod_p, mod_s))

    c_f = 3 * aw
    c_a = c_f + n_heads
    c_gl = c_a + ch
    w_f = jnp.pad(w_in[:, :, c_f:c_a], ((0, 0), (0, 0), (0, LANES - n_heads)))
    b_f3 = jnp.pad(b_f, ((0, 0), (0, LANES - n_heads))).reshape(depth, 1, LANES)
    w_a = w_in[:, :, c_a:c_gl]
    w_gl = w_in[:, :, c_gl:]
    ln_g4 = ln_g.reshape(depth, 3, 1, d)
    ln_b4 = ln_b.reshape(depth, 3, 1, d)
    b_dw3, g_cn3, b_cn3 = (v.reshape(depth, 1, ch) for v in (b_dw, g_cn, b_cn))
    eye_h = jnp.eye(n_heads, dtype=bool)

    x = (x_prompt.reshape(m_p, d), x_sample.reshape(m_s, d))
    xm = _modulate(x[0], x[1], mods[0][0], mods[0][1], 0, tr, seq)

    def cols(w, first, n_blocks, pieces):
        return [[_w(w, col_off=first + q, col_mul=pieces, col_max=first + n_blocks - 1)] for q in range(pieces)]

    def ffn(x, xm, w_in_l, w_down_l, l, which_gate, which_ln, next_mod):
        (h,) = _mm([(xm, d, 0)], cols(w_in_l, 0, nb_ff, 2) + cols(w_in_l, nb_ff, nb_ff, 2), tn_ff, 2 * tn_ff,
                   d_ff, _epi_swiglu, [BF16], tm, l, name="ffn_in")
        y = None
        for kb in range(k_split):
            (y,) = _mm([(h, d_ff // k_split, kb)], [[_w(w_down_l, row_block=kb)]], tn_d, tn_d, d, _epi_plain,
                       [F32], tm, l, add=y, name="ffn_down")
        return _ln_mod(x, y, mods[l][0], mods[l][1], which_gate, 0.5, alpha, ln_g4, ln_b4, l, which_ln,
                       next_mod, tr, seq)

    outs = {k: [] for k in ("kp", "vp", "lfp", "cvp", "ks", "vs", "lfs", "cvs")}
    for l in range(depth):
        mod_p, mod_s = mods[l]
        x, xm = ffn(x, xm, w_ffn1_in, w_ffn1_down, l, 2, 0, (mod_p, mod_s, 3))

        xin = [(xm, d, 0)]
        (q,) = _mm(xin, cols(w_in, 0, nb_aw, 4), tn_aw, 4 * tn_aw, aw, _epi_plain, [BF16], tm, l, name="proj_q")
        (k,) = _mm(xin, cols(w_in, nb_aw, nb_aw, 4), tn_aw, 4 * tn_aw, aw, _epi_plain, [F32], tm, l,
                   name="proj_k")
        (v,) = _mm(xin, cols(w_in, 2 * nb_aw, nb_aw, 4), tn_aw, 4 * tn_aw, aw, _epi_plain, [F32], tm, l,
                   name="proj_v")
        (lf,) = _mm(xin, [[_w(w_f)]], LANES, LANES, LANES, _epi_logsig, [F32], tm, l, bias=b_f3,
                    name="proj_f")
        (u,) = _mm(xin, cols(w_a, 0, nb_ch, 2) + cols(w_gl, 0, nb_ch, 2), tn_ch, 2 * tn_ch, ch, _epi_glu,
                   [F32], tm, l, name="proj_glu")

        attn_p = _prompt_attention(q[0], k[0], v[0], lf[0], batch, seq, n_heads, head_dim, tq)
        conv_p = _prompt_conv(u[0], w_dw, b_dw3, g_cn3, b_cn3, l, batch, seq, tt)

        q_s = q[1].reshape(bs, t_new, n_heads, head_dim).transpose(0, 2, 3, 1)
        wq = jnp.where(eye_h[None, :, None, :, None], q_s[:, :, :, None, :], 0).reshape(
            bs, aw, n_heads * t_new)
        lfn = lf[1][:, :n_heads].reshape(bs, t_new, n_heads)
        o_s = _sample_attention(page_table, wq, k[1].reshape(bs, t_new, aw), v[1].reshape(bs, t_new, aw),
                                lfn, cache_k, cache_v, cache_logf, l, g_pages, n_heads, head_dim, t_new)
        attn_s = o_s.reshape(bs, n_heads, t_new, head_dim).transpose(0, 2, 1, 3).reshape(m_s, aw).astype(BF16)
        u_s = u[1].reshape(bs, t_new, ch)
        ext_s = jnp.concatenate([state_conv[l], u_s, jnp.zeros((bs, -(width - 1 + t_new) % 8, ch), F32)],
                                axis=1)
        conv_s = _sample_conv(ext_s, w_dw, b_dw3, g_cn3, b_cn3, l, t_new).astype(BF16)

        assert aw == ch
        nb_d = d // tn_d
        (y,) = _mm([((attn_p, attn_s), aw, 0), ((conv_p, conv_s), ch, 0)],
                   [[_w(w_out, row_block=0, col_off=q, col_mul=2, col_max=nb_d - 1),
                     _w(w_out, row_block=1, col_off=q, col_mul=2, col_max=nb_d - 1)] for q in range(2)],
                   tn_d, 2 * tn_d, d, _epi_plain, [F32], tm, l, name="proj_out")
        x, xm = _ln_mod(x, y, mod_p, mod_s, 5, 1.0, alpha, ln_g4, ln_b4, l, 1, (mod_p, mod_s, 6), tr, seq)

        next_mod = (mods[l + 1][0], mods[l + 1][1], 0) if l + 1 < depth else None
        x, xm = ffn(x, xm, w_ffn2_in, w_ffn2_down, l, 8, 2, next_mod)

        outs["kp"].append(k[0].reshape(batch, seq, n_heads, head_dim))
        outs["vp"].append(v[0].reshape(batch, seq, n_heads, head_dim))
        outs["lfp"].append(lf[0][:, :n_heads].reshape(batch, seq, n_heads))
        outs["cvp"].append(u[0].reshape(batch, seq, ch)[:, seq - (width - 1):])
        outs["ks"].append(k[1].reshape(bs, t_new, n_heads, head_dim))
        outs["vs"].append(v[1].reshape(bs, t_new, n_heads, head_dim))
        outs["lfs"].append(lfn)
        outs["cvs"].append(jnp.concatenate([state_conv[l], u_s], axis=1)[:, -(width - 1):])

    return (x[0].reshape(batch, seq, d), x[1].reshape(bs, t_new, d),
            jnp.stack(outs["kp"]), jnp.stack(outs["vp"]), jnp.stack(outs["lfp"]), jnp.stack(outs["cvp"]),
            jnp.stack(outs["ks"]), jnp.stack(outs["vs"]), jnp.stack(outs["lfs"]), jnp.stack(outs["cvs"]))
```

```python
import functools

import jax
import jax.numpy as jnp
from jax import lax
from jax.experimental import pallas as pl
from jax.experimental.pallas import tpu as pltpu

F32 = jnp.float32
BF16 = jnp.bfloat16
LN_EPS = 1e-5
N_MOD = 9
LANES = 128
SUBLANES = 8
VMEM_LIMIT_BYTES = 60 * 1024 * 1024


def _params(*sem):
    return pltpu.CompilerParams(dimension_semantics=sem, vmem_limit_bytes=VMEM_LIMIT_BYTES)


def _split3(x):
    hi = x.astype(BF16)
    r1 = x - hi.astype(F32)
    mid = r1.astype(BF16)
    lo = (r1 - mid.astype(F32)).astype(BF16)
    return hi, mid, lo


def _dot01_left(m01, x):
    return sum(jnp.dot(m01, t, preferred_element_type=F32) for t in _split3(x))


def _dot01_right(x, m01):
    return sum(jnp.dot(t, m01, preferred_element_type=F32) for t in _split3(x))


def _adaln_kernel(c_ref, w_ref, b_ref, o_ref):
    c = c_ref[...]
    x = (c * jax.nn.sigmoid(c)).astype(BF16)
    o_ref[...] = jnp.dot(x, w_ref[...].astype(BF16), preferred_element_type=F32) + b_ref[...]


def _adaln(c_all, w_ada, b_ada3, layer, tn):
    rows, d = c_all.shape
    n = w_ada.shape[2]
    return pl.pallas_call(
        _adaln_kernel,
        grid=(n // tn,),
        in_specs=[pl.BlockSpec((rows, d), lambda j: (0, 0)),
                  pl.BlockSpec((None, d, tn), lambda j: (layer, 0, j)),
                  pl.BlockSpec((None, 1, tn), lambda j: (layer, 0, j))],
        out_specs=pl.BlockSpec((rows, tn), lambda j: (0, j)),
        out_shape=jax.ShapeDtypeStruct((rows, n), F32),
        compiler_params=_params("arbitrary"),
        name="adaln",
    )(c_all, w_ada, b_ada3)


def _epi_plain(accs, bias):
    return [jnp.concatenate(accs, axis=1) if len(accs) > 1 else accs[0]]


def _epi_swiglu(accs, bias):
    half = len(accs) // 2
    g = jnp.concatenate(accs[:half], axis=1)
    u = jnp.concatenate(accs[half:], axis=1)
    return [g * jax.nn.sigmoid(g) * u]


def _epi_glu(accs, bias):
    half = len(accs) // 2
    a = jnp.concatenate(accs[:half], axis=1)
    gl = jnp.concatenate(accs[half:], axis=1)
    return [a * jax.nn.sigmoid(gl)]


def _epi_logsig(accs, bias):
    z = accs[0] + bias
    return [jnp.minimum(z, 0.0) - jnp.log1p(jnp.exp(-jnp.abs(z)))]


def _mm_kernel(*refs, n_x, n_br, n_out, n_p, has_bias, has_add, epilogue):
    it = iter(refs)
    x_p = [next(it) for _ in range(n_x)]
    x_s = [next(it) for _ in range(n_x)]
    w = [[next(it) for _ in range(n_x)] for _ in range(n_br)]
    bias = next(it) if has_bias else None
    add_p = next(it) if has_add else None
    add_s = next(it) if has_add else None
    o_p = [next(it) for _ in range(n_out)]
    o_s = [next(it) for _ in range(n_out)]
    wb = [[next(it) for _ in range(n_x)] for _ in range(n_br)]
    i = pl.program_id(1)

    @pl.when(i == 0)
    def _():
        for b in range(n_br):
            for p in range(n_x):
                wb[b][p][...] = w[b][p][...].astype(BF16)

    def run(xs, add, outs):
        accs = []
        for b in range(n_br):
            acc = None
            for p in range(n_x):
                d = jnp.dot(xs[p][...], wb[b][p][...], preferred_element_type=F32)
                acc = d if acc is None else acc + d
            accs.append(acc)
        if has_add:
            accs[0] = accs[0] + add[...]
        vals = epilogue(accs, bias[...] if has_bias else None)
        for o, v in zip(outs, vals):
            o[...] = v.astype(o.dtype)

    @pl.when(i < n_p)
    def _():
        run(x_p, add_p, o_p)

    @pl.when(i == n_p)
    def _():
        run(x_s, add_s, o_s)


def _mm(xs, w_specs, w_tn, out_tn, n_cols, epilogue, out_dtypes, tm, layer, bias=None, add=None, name="mm"):
    n_x, n_br, n_out = len(xs), len(w_specs), len(out_dtypes)
    m_p, m_s = xs[0][0][0].shape[0], xs[0][0][1].shape[0]
    n_p = m_p // tm
    grid = (pl.cdiv(n_cols, out_tn), n_p + 1)
    in_specs, args = [], []
    for (x_p, _), kp, kb in xs:
        in_specs.append(pl.BlockSpec((tm, kp), lambda j, i, kb=kb: (jnp.minimum(i, n_p - 1), kb)))
        args.append(x_p)
    for (_, x_s), kp, kb in xs:
        in_specs.append(pl.BlockSpec((m_s, kp), lambda j, i, kb=kb: (0, kb)))
        args.append(x_s)
    scratch = []
    for b in range(n_br):
        for p in range(n_x):
            w, rb, cm, co, cmax = w_specs[b][p]
            kp = xs[p][1]
            in_specs.append(pl.BlockSpec(
                (None, kp, w_tn),
                lambda j, i, rb=rb, cm=cm, co=co, cmax=cmax: (layer, rb, jnp.minimum(cm * j + co, cmax))))
            args.append(w)
            scratch.append(pltpu.VMEM((kp, w_tn), BF16))
    if bias is not None:
        in_specs.append(pl.BlockSpec((None, 1, out_tn), lambda j, i: (layer, 0, j)))
        args.append(bias)
    out_p = pl.BlockSpec((tm, out_tn), lambda j, i: (jnp.minimum(i, n_p - 1), j))
    out_s = pl.BlockSpec((m_s, out_tn), lambda j, i: (0, j))
    if add is not None:
        in_specs += [out_p, out_s]
        args += [add[0], add[1]]
    out_specs = [out_p for _ in out_dtypes] + [out_s for _ in out_dtypes]
    out_shape = ([jax.ShapeDtypeStruct((m_p, n_cols), dt) for dt in out_dtypes]
                 + [jax.ShapeDtypeStruct((m_s, n_cols), dt) for dt in out_dtypes])
    outs = pl.pallas_call(
        functools.partial(_mm_kernel, n_x=n_x, n_br=n_br, n_out=n_out, n_p=n_p,
                          has_bias=bias is not None, has_add=add is not None, epilogue=epilogue),
        grid=grid, in_specs=in_specs, out_specs=out_specs, out_shape=out_shape,
        scratch_shapes=scratch,
        compiler_params=_params("arbitrary", "arbitrary"),
        name=name,
    )(*args)
    return [(outs[k], outs[n_out + k]) for k in range(n_out)]


def _w(w, row_block=0, col_off=0, col_mul=1, col_max=2 ** 30):
    return (w, row_block, col_mul, col_off, col_max)


def _shift_cols_kernel(a_ref, b_ref, o_ref, *, shift):
    cat = jnp.concatenate([a_ref[...], b_ref[...]], axis=1)
    o_ref[...] = pltpu.roll(cat, cat.shape[1] - shift, axis=1)[:, :o_ref.shape[1]]


def _shift_cols(w, start, n_cols, tr):
    depth, rows, total = w.shape
    shift = start % LANES
    base = start - shift
    wb = max(t for t in (512, 256, LANES) if base % t == 0 and n_cols % t == 0)
    tr = _largest_divisor(rows, tr, SUBLANES)
    last_lane_block = (total - 1) // LANES
    return pl.pallas_call(
        functools.partial(_shift_cols_kernel, shift=shift),
        grid=(depth, rows // tr, n_cols // wb),
        in_specs=[pl.BlockSpec((None, tr, wb), lambda l, i, j: (l, i, base // wb + j)),
                  pl.BlockSpec((None, tr, LANES),
                               lambda l, i, j: (l, i, jnp.minimum((base // wb + j + 1) * (wb // LANES),
                                                                  last_lane_block)))],
        out_specs=pl.BlockSpec((None, tr, wb), lambda l, i, j: (l, i, j)),
        out_shape=jax.ShapeDtypeStruct((depth, rows, n_cols), w.dtype),
        compiler_params=_params("arbitrary", "arbitrary", "arbitrary"),
        name="shift_cols",
    )(w, w)


def _mod_kernel(x_p, x_s, sh_p, sc_p, sh_s, sc_s, o_p, o_s, *, n_p):
    i = pl.program_id(0)

    @pl.when(i < n_p)
    def _():
        o_p[...] = (x_p[...] * (1.0 + sc_p[...]) + sh_p[...]).astype(o_p.dtype)

    @pl.when(i == n_p)
    def _():
        o_s[...] = (x_s[...] * (1.0 + sc_s[...]) + sh_s[...]).astype(o_s.dtype)


def _mod_specs(n_p, steps_per_seq, which, d, m_s):
    p = pl.BlockSpec((None, None, 1, d),
                     lambda i, w=which: (jnp.minimum(i, n_p - 1) // steps_per_seq, w, 0, 0))
    s = pl.BlockSpec((None, m_s, d), lambda i, w=which: (w, 0, 0))
    return p, s


def _modulate(x_p, x_s, mod_p, mod_s, which_shift, tr, seq):
    m_p, d = x_p.shape
    m_s = x_s.shape[0]
    n_p = m_p // tr
    spp = seq // tr
    sh_p, sh_s = _mod_specs(n_p, spp, which_shift, d, m_s)
    sc_p, sc_s = _mod_specs(n_p, spp, which_shift + 1, d, m_s)
    row_p = pl.BlockSpec((tr, d), lambda i: (jnp.minimum(i, n_p - 1), 0))
    row_s = pl.BlockSpec((m_s, d), lambda i: (0, 0))
    return pl.pallas_call(
        functools.partial(_mod_kernel, n_p=n_p),
        grid=(n_p + 1,),
        in_specs=[row_p, row_s, sh_p, sc_p, sh_s, sc_s],
        out_specs=[row_p, row_s],
        out_shape=[jax.ShapeDtypeStruct((m_p, d), BF16), jax.ShapeDtypeStruct((m_s, d), BF16)],
        compiler_params=_params("arbitrary"),
        name="modulate",
    )(x_p, x_s, mod_p, mod_p, mod_s, mod_s)


def _ln_kernel(*refs, n_p, alpha, coef, with_mod):
    if with_mod:
        (x_p, y_p, x_s, y_s, g_p, g_s, sh_p, sc_p, sh_s, sc_s, lng, lnb, o_p, o_s, m_p, m_s) = refs
    else:
        (x_p, y_p, x_s, y_s, g_p, g_s, lng, lnb, o_p, o_s) = refs
    i = pl.program_id(0)

    def run(x, y, gate, shift, scale, o, mo):
        z = alpha * x[...] + coef * gate[...] * y[...]
        mu = jnp.mean(z, axis=-1, keepdims=True)
        zc = z - mu
        var = jnp.mean(zc * zc, axis=-1, keepdims=True)
        out = zc * lax.rsqrt(var + LN_EPS) * lng[...] + lnb[...]
        o[...] = out
        if with_mod:
            mo[...] = (out * (1.0 + scale[...]) + shift[...]).astype(mo.dtype)

    @pl.when(i < n_p)
    def _():
        run(x_p, y_p, g_p, sh_p if with_mod else None, sc_p if with_mod else None, o_p,
            m_p if with_mod else None)

    @pl.when(i == n_p)
    def _():
        run(x_s, y_s, g_s, sh_s if with_mod else None, sc_s if with_mod else None, o_s,
            m_s if with_mod else None)


def _ln_mod(x, y, mod_p, mod_s, which_gate, coef, alpha, ln_g, ln_b, layer, which_ln, next_mod, tr, seq):
    (x_p, x_s), (y_p, y_s) = x, y
    m_p, d = x_p.shape
    m_s = x_s.shape[0]
    n_p = m_p // tr
    spp = seq // tr
    row_p = pl.BlockSpec((tr, d), lambda i: (jnp.minimum(i, n_p - 1), 0))
    row_s = pl.BlockSpec((m_s, d), lambda i: (0, 0))
    g_p, g_s = _mod_specs(n_p, spp, which_gate, d, m_s)
    ln_spec = pl.BlockSpec((None, None, 1, d), lambda i: (layer, which_ln, 0, 0))
    in_specs = [row_p, row_p, row_s, row_s, g_p, g_s]
    args = [x_p, y_p, x_s, y_s, mod_p, mod_s]
    out_specs = [row_p, row_s]
    out_shape = [jax.ShapeDtypeStruct((m_p, d), F32), jax.ShapeDtypeStruct((m_s, d), F32)]
    if next_mod is not None:
        nm_p, nm_s, ws = next_mod
        sh_p, sh_s = _mod_specs(n_p, spp, ws, d, m_s)
        sc_p, sc_s = _mod_specs(n_p, spp, ws + 1, d, m_s)
        in_specs += [sh_p, sc_p, sh_s, sc_s]
        args += [nm_p, nm_p, nm_s, nm_s]
        out_specs += [row_p, row_s]
        out_shape += [jax.ShapeDtypeStruct((m_p, d), BF16), jax.ShapeDtypeStruct((m_s, d), BF16)]
    in_specs += [ln_spec, ln_spec]
    args += [ln_g, ln_b]
    outs = pl.pallas_call(
        functools.partial(_ln_kernel, n_p=n_p, alpha=alpha, coef=coef, with_mod=next_mod is not None),
        grid=(n_p + 1,), in_specs=in_specs, out_specs=out_specs, out_shape=out_shape,
        compiler_params=_params("arbitrary"),
        name="ln_mod",
    )(*args)
    if next_mod is None:
        return (outs[0], outs[1]), None
    return (outs[0], outs[1]), (outs[2], outs[3])


def _pattn_kernel(q_ref, k_ref, v_ref, lf_ref, o_ref, kb, vb, cum, cum_t, *, tq, seq, scale):
    h = pl.program_id(1)
    qi = pl.program_id(2)
    n_blk = seq // tq

    @pl.when(qi == 0)
    def _():
        kb[...] = k_ref[...].astype(BF16)
        vb[...] = v_ref[...].astype(BF16)

    @pl.when((qi == 0) & (h == 0))
    def _():
        r = lax.broadcasted_iota(jnp.int32, (LANES, LANES), 0)
        c = lax.broadcasted_iota(jnp.int32, (LANES, LANES), 1)
        tri = (r >= c).astype(BF16)
        carry = jnp.zeros((1, LANES), F32)
        for blk in range(seq // LANES):
            x = lf_ref[blk * LANES:(blk + 1) * LANES, :]
            cb = _dot01_left(tri, x) + carry
            cum[blk * LANES:(blk + 1) * LANES, :] = cb
            carry = cb[LANES - 1:LANES, :]
        for blk in range(n_blk):
            cum_t[blk] = cum[blk * tq:(blk + 1) * tq, :].T

    q = q_ref[...]
    lane = lax.broadcasted_iota(jnp.int32, (tq, LANES), 1)
    cq = jnp.sum(jnp.where(lane == h, cum[pl.ds(pl.multiple_of(qi * tq, tq), tq), :], 0.0),
                 axis=1, keepdims=True)

    def block(ki, carry, masked):
        m, l, acc = carry
        start = pl.multiple_of(ki * tq, tq)
        s = lax.dot_general(q, kb[pl.ds(start, tq), :], (((1,), (1,)), ((), ())),
                            preferred_element_type=F32) * scale
        ck = cum_t[ki, pl.ds(h, 1), :]
        s = s - ck
        if masked:
            row = lax.broadcasted_iota(jnp.int32, (tq, tq), 0)
            col = lax.broadcasted_iota(jnp.int32, (tq, tq), 1)
            s = jnp.where(col <= row, s, -jnp.inf)
        m_new = jnp.maximum(m, jnp.max(s, axis=1, keepdims=True) + cq)
        a = jnp.exp(m - m_new)
        p = jnp.exp(s - (m_new - cq))
        l = a * l + jnp.sum(p, axis=1, keepdims=True)
        acc = a * acc + jnp.dot(p.astype(BF16), vb[pl.ds(start, tq), :], preferred_element_type=F32)
        return m_new, l, acc

    init = (jnp.full((tq, 1), -jnp.inf, F32), jnp.zeros((tq, 1), F32), jnp.zeros((tq, q_ref.shape[1]), F32))
    carry = lax.fori_loop(0, qi, lambda ki, cr: block(ki, cr, False), init)
    m, l, acc = block(qi, carry, True)
    o_ref[...] = (acc / l).astype(o_ref.dtype)


def _prompt_attention(q_p, k_p, v_p, lf_p, batch, seq, n_heads, head_dim, tq):
    m_p = q_p.shape[0]
    nq = seq // tq
    return pl.pallas_call(
        functools.partial(_pattn_kernel, tq=tq, seq=seq, scale=head_dim ** -0.5),
        grid=(batch, n_heads, nq),
        in_specs=[pl.BlockSpec((tq, head_dim), lambda b, h, qi: (b * nq + qi, h)),
                  pl.BlockSpec((seq, head_dim), lambda b, h, qi: (b, h)),
                  pl.BlockSpec((seq, head_dim), lambda b, h, qi: (b, h)),
                  pl.BlockSpec((seq, LANES), lambda b, h, qi: (b, 0))],
        out_specs=pl.BlockSpec((tq, head_dim), lambda b, h, qi: (b * nq + qi, h)),
        out_shape=jax.ShapeDtypeStruct((m_p, n_heads * head_dim), BF16),
        scratch_shapes=[pltpu.VMEM((seq, head_dim), BF16), pltpu.VMEM((seq, head_dim), BF16),
                        pltpu.VMEM((seq, LANES), F32), pltpu.VMEM((nq, LANES, tq), F32)],
        compiler_params=_params("arbitrary", "arbitrary", "arbitrary"),
        name="prompt_attention",
    )(q_p, k_p, v_p, lf_p)


def _sattn_kernel(pt_ref, wq_ref, kn_ref, vn_ref, lfn_ref, *refs, g_pages, page, n_heads, t_new, scale):
    k_refs = refs[:g_pages]
    v_refs = refs[g_pages:2 * g_pages]
    lf_refs = refs[2 * g_pages:3 * g_pages]
    o_ref = refs[3 * g_pages]
    kb, vb, lfbuf, rtbuf, crow_s, m_s, l_s, acc_s, carry_s = refs[3 * g_pages + 1:]
    st = pl.program_id(1)
    n_st = pl.num_programs(1)
    ht = n_heads * t_new
    hd = kb.shape[1]
    d = hd // n_heads
    wq = wq_ref[...]

    expand = (lax.broadcasted_iota(jnp.int32, (LANES, ht), 0)
              == lax.broadcasted_iota(jnp.int32, (LANES, ht), 1) // t_new).astype(BF16)
    key_h = lax.broadcasted_iota(jnp.int32, (n_heads, page), 1)
    key_q = lax.broadcasted_iota(jnp.int32, (page, ht), 0)
    lane_t = lax.broadcasted_iota(jnp.int32, (page, ht), 1) % t_new

    def update(s_t, v_chunk):
        m_old = m_s[0:1, :]
        m_new = jnp.maximum(m_old, jnp.max(s_t, axis=0, keepdims=True))
        a = jnp.exp(m_old - m_new)
        p = jnp.exp(s_t - m_new)
        l_s[...] = jnp.broadcast_to(a * l_s[0:1, :] + jnp.sum(p, axis=0, keepdims=True), l_s.shape)
        m_s[...] = jnp.broadcast_to(m_new, m_s.shape)
        o_full = jnp.dot(p.T.astype(BF16), v_chunk, preferred_element_type=F32)
        o_diag = jnp.concatenate(
            [o_full[hh * t_new:(hh + 1) * t_new, hh * d:(hh + 1) * d] for hh in range(n_heads)], axis=0)
        a_col = jnp.broadcast_to(a, (ht, ht)).T[:, 0:1]
        acc_s[...] = a_col * acc_s[...] + o_diag

    @pl.when(st == 0)
    def _():
        lfbuf[...] = jnp.zeros(lfbuf.shape, F32)
        rtbuf[...] = jnp.zeros(rtbuf.shape, F32)
        carry_s[...] = jnp.zeros(carry_s.shape, F32)
        m_s[...] = jnp.full(m_s.shape, -jnp.inf, F32)
        l_s[...] = jnp.zeros(l_s.shape, F32)
        acc_s[...] = jnp.zeros(acc_s.shape, F32)
        pad = jnp.zeros((page - t_new, hd), F32)
        kn = jnp.concatenate([kn_ref[...], pad], axis=0).astype(BF16)
        vn = jnp.concatenate([vn_ref[...], pad], axis=0).astype(BF16)
        lfbuf[0:t_new, 0:n_heads] = lfn_ref[...]
        tri = (lax.broadcasted_iota(jnp.int32, (page, page), 0)
               >= lax.broadcasted_iota(jnp.int32, (page, page), 1)).astype(BF16)
        c_nat = _dot01_left(tri, lfbuf[0:page, :])
        c_exp = _dot01_right(c_nat, expand)
        c_row = jnp.sum(jnp.where(key_q == lane_t, c_exp, 0.0), axis=0, keepdims=True)
        crow_s[...] = jnp.broadcast_to(c_row, crow_s.shape)
        s_t = jnp.dot(kn, wq, preferred_element_type=F32) * scale
        s_t = s_t + (c_row - c_exp)
        s_t = jnp.where(key_q <= lane_t, s_t, -jnp.inf)
        update(s_t, vn)

    c_row = crow_s[0:1, :]
    for g in range(g_pages):
        for hh in range(n_heads):
            kb[g * page:(g + 1) * page, hh * d:(hh + 1) * d] = (
                k_refs[g][pl.ds(hh, page, stride=n_heads), :].astype(BF16))
            vb[g * page:(g + 1) * page, hh * d:(hh + 1) * d] = (
                v_refs[g][pl.ds(hh, page, stride=n_heads), :].astype(BF16))
    carry = carry_s[...]
    for g in reversed(range(g_pages)):
        y = lf_refs[g][...]
        shift = 1
        while shift < page:
            y = y + jnp.where(key_h + shift < page, pltpu.roll(y, page - shift, axis=1), 0.0)
            shift *= 2
        excl = jnp.where(key_h + 1 < page, pltpu.roll(y, page - 1, axis=1), 0.0)
        rtbuf[g, 0:n_heads, :] = excl + carry
        carry = carry + jnp.broadcast_to(y[:, 0:1], carry.shape)
    carry_s[...] = carry
    r_nat = jnp.concatenate([rtbuf[g].T for g in range(g_pages)], axis=0)
    r_exp = _dot01_right(r_nat, expand)
    s_t = jnp.dot(kb[...], wq, preferred_element_type=F32) * scale
    s_t = s_t + (c_row + r_exp)
    update(s_t, vb[...])

    @pl.when(st == n_st - 1)
    def _():
        l_col = jnp.broadcast_to(l_s[0:1, :], (ht, ht)).T[:, 0:1]
        o_ref[...] = acc_s[...] / l_col


def _sample_attention(page_table, wq, kn, vn, lfn, cache_k, cache_v, cache_logf, layer, g_pages,
                      n_heads, head_dim, t_new):
    bs, n_pages = page_table.shape
    page = cache_k.shape[2]
    hd = n_heads * head_dim
    ht = n_heads * t_new
    n_st = n_pages // g_pages
    ck = cache_k.reshape(cache_k.shape[0], cache_k.shape[1], page * n_heads, head_dim)
    cv = cache_v.reshape(cache_v.shape[0], cache_v.shape[1], page * n_heads, head_dim)
    clf = cache_logf.transpose(0, 1, 3, 2)

    def page_map(g):
        def f(b, st, pt):
            return (layer, pt[b * n_pages + (n_st - 1 - st) * g_pages + g], 0, 0)
        return f

    in_specs = [pl.BlockSpec((None, hd, ht), lambda b, st, pt: (b, 0, 0)),
                pl.BlockSpec((None, t_new, hd), lambda b, st, pt: (b, 0, 0)),
                pl.BlockSpec((None, t_new, hd), lambda b, st, pt: (b, 0, 0)),
                pl.BlockSpec((None, t_new, n_heads), lambda b, st, pt: (b, 0, 0))]
    in_specs += [pl.BlockSpec((None, None, page * n_heads, head_dim), page_map(g)) for g in range(g_pages)]
    in_specs += [pl.BlockSpec((None, None, page * n_heads, head_dim), page_map(g)) for g in range(g_pages)]
    in_specs += [pl.BlockSpec((None, None, n_heads, page), page_map(g)) for g in range(g_pages)]
    grid_spec = pltpu.PrefetchScalarGridSpec(
        num_scalar_prefetch=1, grid=(bs, n_st), in_specs=in_specs,
        out_specs=pl.BlockSpec((None, ht, head_dim), lambda b, st, pt: (b, 0, 0)),
        scratch_shapes=[pltpu.VMEM((g_pages * page, hd), BF16), pltpu.VMEM((g_pages * page, hd), BF16),
                        pltpu.VMEM((page, LANES), F32), pltpu.VMEM((g_pages, LANES, page), F32),
                        pltpu.VMEM((8, ht), F32), pltpu.VMEM((8, ht), F32), pltpu.VMEM((8, ht), F32),
                        pltpu.VMEM((ht, head_dim), F32), pltpu.VMEM((n_heads, page), F32)])
    return pl.pallas_call(
        functools.partial(_sattn_kernel, g_pages=g_pages, page=page, n_heads=n_heads, t_new=t_new,
                          scale=head_dim ** -0.5),
        grid_spec=grid_spec,
        out_shape=jax.ShapeDtypeStruct((bs, ht, head_dim), F32),
        compiler_params=_params("arbitrary", "arbitrary"),
        name="sample_attention",
    )(page_table.reshape(-1), wq, kn, vn, lfn, *([ck] * g_pages), *([cv] * g_pages),
      *([clf] * g_pages))


def _conv_rows(window, w_ref, bdw, gcn, bcn, width, group):
    ch = w_ref.shape[1]
    parts = []
    for c0 in range(0, ch, group):
        acc = None
        for j in range(width):
            term = w_ref[j:j + 1, c0:c0 + group] * window(j, c0, c0 + group)
            acc = term if acc is None else acc + term
        parts.append(acc)
    y = jnp.concatenate(parts, axis=1) + bdw
    mu = jnp.mean(y, axis=-1, keepdims=True)
    yc = y - mu
    var = jnp.mean(yc * yc, axis=-1, keepdims=True)
    z = yc * lax.rsqrt(var + LN_EPS) * gcn + bcn
    return z * jax.nn.sigmoid(z)


def _pconv_kernel(u_ref, halo_ref, w_ref, bdw_ref, gcn_ref, bcn_ref, o_ref, ext, *, tt, halo, width, chunk,
                  group):
    ti = pl.program_id(1)
    n_ext = halo + tt
    ext[0, 0:halo, :] = jnp.where(ti > 0, halo_ref[...], 0.0)
    ext[0, halo:n_ext, :] = u_ref[...]
    for r in range(1, SUBLANES):
        ext[r, 0:n_ext - SUBLANES, :] = ext[0, r:r + n_ext - SUBLANES, :]
    bdw, gcn, bcn = bdw_ref[...], gcn_ref[...], bcn_ref[...]
    first = halo - (width - 1)

    def body(c, _):
        r0 = pl.multiple_of(c * chunk, chunk)

        def window(j, c0, c1):
            off = first + j
            return ext[off % SUBLANES, pl.ds(r0 + (off // SUBLANES) * SUBLANES, chunk), c0:c1]

        z = _conv_rows(window, w_ref, bdw, gcn, bcn, width, group)
        o_ref[pl.ds(r0, chunk), :] = z.astype(o_ref.dtype)
        return 0

    lax.fori_loop(0, tt // chunk, body, 0)


def _prompt_conv(u_p, w_dw, b_dw, g_cn, b_cn, layer, batch, seq, tt, chunk, group):
    m_p, ch = u_p.shape
    width = w_dw.shape[1]
    halo = 32
    assert width - 1 <= halo and tt % halo == 0
    nt = seq // tt
    vec = pl.BlockSpec((None, 1, ch), lambda b, ti: (layer, 0, 0))
    return pl.pallas_call(
        functools.partial(_pconv_kernel, tt=tt, halo=halo, width=width, chunk=chunk, group=group),
        grid=(batch, nt),
        in_specs=[pl.BlockSpec((tt, ch), lambda b, ti: (b * nt + ti, 0)),
                  pl.BlockSpec((halo, ch), lambda b, ti: (jnp.maximum((b * nt + ti) * (tt // halo) - 1, 0), 0)),
                  pl.BlockSpec((None, width, ch), lambda b, ti: (layer, 0, 0)),
                  vec, vec, vec],
        out_specs=pl.BlockSpec((tt, ch), lambda b, ti: (b * nt + ti, 0)),
        out_shape=jax.ShapeDtypeStruct((m_p, ch), BF16),
        scratch_shapes=[pltpu.VMEM((SUBLANES, halo + tt, ch), F32)],
        compiler_params=_params("arbitrary", "arbitrary"),
        name="prompt_conv",
    )(u_p, u_p, w_dw, b_dw, g_cn, b_cn)


def _sconv_kernel(ext_ref, w_ref, bdw_ref, gcn_ref, bcn_ref, o_ref, *, n_seq, t_new, width):
    bdw, gcn, bcn = bdw_ref[...], gcn_ref[...], bcn_ref[...]
    for b in range(n_seq):
        z = _conv_rows(lambda j, c0, c1, b=b: ext_ref[b, j:j + t_new, c0:c1], w_ref, bdw, gcn, bcn, width,
                       w_ref.shape[1])
        o_ref[b * t_new:(b + 1) * t_new, :] = z


def _sample_conv(ext_s, w_dw, b_dw, g_cn, b_cn, layer, t_new):
    n_seq, rows, ch = ext_s.shape
    width = w_dw.shape[1]
    vec = pl.BlockSpec((None, 1, ch), lambda i: (layer, 0, 0))
    return pl.pallas_call(
        functools.partial(_sconv_kernel, n_seq=n_seq, t_new=t_new, width=width),
        grid=(1,),
        in_specs=[pl.BlockSpec((n_seq, rows, ch), lambda i: (0, 0, 0)),
                  pl.BlockSpec((None, width, ch), lambda i: (layer, 0, 0)),
                  vec, vec, vec],
        out_specs=pl.BlockSpec((n_seq * t_new, ch), lambda i: (0, 0)),
        out_shape=jax.ShapeDtypeStruct((n_seq * t_new, ch), F32),
        compiler_params=_params("arbitrary"),
        name="sample_conv",
    )(ext_s, w_dw, b_dw, g_cn, b_cn)


def _largest_divisor(n, cap, multiple):
    best = None
    for t in range(multiple, min(n, cap) + 1, multiple):
        if n % t == 0:
            best = t
    assert best is not None, (n, cap, multiple)
    return best


def kernel(x_prompt, x_sample, c_prompt, c_sample, cache_k, cache_v, cache_logf, state_conv, page_table,
           w_ada, b_ada, w_ffn1_in, w_ffn1_down, w_in, b_f, w_dw, b_dw, g_cn, b_cn, w_out,
           w_ffn2_in, w_ffn2_down, ln_g, ln_b):
    batch, seq, d = x_prompt.shape
    bs, t_new, _ = x_sample.shape
    depth = w_ada.shape[0]
    n_heads, head_dim = cache_k.shape[3], cache_k.shape[4]
    aw = n_heads * head_dim
    ch = d - aw
    d_ff = w_ffn1_down.shape[1]
    width = w_dw.shape[1]
    alpha = (2 * depth) ** 0.25
    m_p, m_s = batch * seq, bs * t_new

    tm = _largest_divisor(m_p, 512, 16)
    tr = _largest_divisor(seq, 256, 16)
    tn_ff = _largest_divisor(d_ff, 256, LANES)
    tn_aw = _largest_divisor(aw, 256, LANES)
    tn_ch = _largest_divisor(ch, 256, LANES)
    tn_d = _largest_divisor(d, 512, LANES)
    nb_ff, nb_aw, nb_ch = d_ff // tn_ff, aw // tn_aw, ch // tn_ch
    k_split = 2 if d_ff % (2 * LANES) == 0 else 1
    tn_ada = _largest_divisor(N_MOD * d, 512, LANES)
    tq = _largest_divisor(seq, 1024, LANES)
    tt = _largest_divisor(seq, 256, 32)
    conv_chunk = _largest_divisor(tt, 64, 16)
    conv_group = _largest_divisor(ch, 128, LANES)
    g_pages = _largest_divisor(page_table.shape[1], 8, 1)

    n_seq_all = batch + bs
    c_all = jnp.concatenate([c_prompt, c_sample, jnp.zeros((-n_seq_all % 8, d), F32)], axis=0)
    b_ada3 = b_ada.reshape(depth, 1, N_MOD * d)
    mods = []
    for l in range(depth):
        m = _adaln(c_all, w_ada, b_ada3, l, tn_ada)
        mod_p = m[:batch].reshape(batch, N_MOD, 1, d)
        mod_s = jnp.repeat(m[batch:n_seq_all].reshape(bs, N_MOD, d).transpose(1, 0, 2), t_new, axis=1)
        mods.append((mod_p, mod_s))

    c_f = 3 * aw
    c_a = c_f + n_heads
    c_gl = c_a + ch
    assert c_f % LANES == 0 and n_heads < LANES
    b_f3 = jnp.pad(b_f, ((0, 0), (0, LANES - n_heads))).reshape(depth, 1, LANES)
    w_a = _shift_cols(w_in, c_a, ch, tm)
    w_gl = _shift_cols(w_in, c_gl, ch, tm)
    ln_g4 = ln_g.reshape(depth, 3, 1, d)
    ln_b4 = ln_b.reshape(depth, 3, 1, d)
    b_dw3, g_cn3, b_cn3 = (v.reshape(depth, 1, ch) for v in (b_dw, g_cn, b_cn))
    eye_h = jnp.eye(n_heads, dtype=bool)

    x = (x_prompt.reshape(m_p, d), x_sample.reshape(m_s, d))
    xm = _modulate(x[0], x[1], mods[0][0], mods[0][1], 0, tr, seq)

    def cols(w, first, n_blocks, pieces):
        return [[_w(w, col_off=first + q, col_mul=pieces, col_max=first + n_blocks - 1)] for q in range(pieces)]

    def ffn(x, xm, w_in_l, w_down_l, l, which_gate, which_ln, next_mod):
        (h,) = _mm([(xm, d, 0)], cols(w_in_l, 0, nb_ff, 2) + cols(w_in_l, nb_ff, nb_ff, 2), tn_ff, 2 * tn_ff,
                   d_ff, _epi_swiglu, [BF16], tm, l, name="ffn_in")
        y = None
        for kb in range(k_split):
            (y,) = _mm([(h, d_ff // k_split, kb)], [[_w(w_down_l, row_block=kb)]], tn_d, tn_d, d, _epi_plain,
                       [F32], tm, l, add=y, name="ffn_down")
        return _ln_mod(x, y, mods[l][0], mods[l][1], which_gate, 0.5, alpha, ln_g4, ln_b4, l, which_ln,
                       next_mod, tr, seq)

    outs = {k: [] for k in ("kp", "vp", "lfp", "cvp", "ks", "vs", "lfs", "cvs")}
    for l in range(depth):
        mod_p, mod_s = mods[l]
        x, xm = ffn(x, xm, w_ffn1_in, w_ffn1_down, l, 2, 0, (mod_p, mod_s, 3))

        xin = [(xm, d, 0)]
        (q,) = _mm(xin, cols(w_in, 0, nb_aw, 4), tn_aw, 4 * tn_aw, aw, _epi_plain, [BF16], tm, l, name="proj_q")
        (k,) = _mm(xin, cols(w_in, nb_aw, nb_aw, 4), tn_aw, 4 * tn_aw, aw, _epi_plain, [F32], tm, l,
                   name="proj_k")
        (v,) = _mm(xin, cols(w_in, 2 * nb_aw, nb_aw, 4), tn_aw, 4 * tn_aw, aw, _epi_plain, [F32], tm, l,
                   name="proj_v")
        (lf,) = _mm(xin, [[_w(w_in, col_off=c_f // LANES)]], LANES, LANES, LANES, _epi_logsig, [F32], tm, l, bias=b_f3,
                    name="proj_f")
        (u,) = _mm(xin, cols(w_a, 0, nb_ch, 2) + cols(w_gl, 0, nb_ch, 2), tn_ch, 2 * tn_ch, ch, _epi_glu,
                   [F32], tm, l, name="proj_glu")

        attn_p = _prompt_attention(q[0], k[0], v[0], lf[0], batch, seq, n_heads, head_dim, tq)
        conv_p = _prompt_conv(u[0], w_dw, b_dw3, g_cn3, b_cn3, l, batch, seq, tt, conv_chunk, conv_group)

        q_s = q[1].reshape(bs, t_new, n_heads, head_dim).transpose(0, 2, 3, 1)
        wq = jnp.where(eye_h[None, :, None, :, None], q_s[:, :, :, None, :], 0).reshape(
            bs, aw, n_heads * t_new)
        lfn = lf[1][:, :n_heads].reshape(bs, t_new, n_heads)
        o_s = _sample_attention(page_table, wq, k[1].reshape(bs, t_new, aw), v[1].reshape(bs, t_new, aw),
                                lfn, cache_k, cache_v, cache_logf, l, g_pages, n_heads, head_dim, t_new)
        attn_s = o_s.reshape(bs, n_heads, t_new, head_dim).transpose(0, 2, 1, 3).reshape(m_s, aw).astype(BF16)
        u_s = u[1].reshape(bs, t_new, ch)
        ext_s = jnp.concatenate([state_conv[l], u_s, jnp.zeros((bs, -(width - 1 + t_new) % 8, ch), F32)],
                                axis=1)
        conv_s = _sample_conv(ext_s, w_dw, b_dw3, g_cn3, b_cn3, l, t_new).astype(BF16)

        assert aw == ch
        nb_d = d // tn_d
        (y,) = _mm([((attn_p, attn_s), aw, 0), ((conv_p, conv_s), ch, 0)],
                   [[_w(w_out, row_block=0, col_off=q, col_mul=2, col_max=nb_d - 1),
                     _w(w_out, row_block=1, col_off=q, col_mul=2, col_max=nb_d - 1)] for q in range(2)],
                   tn_d, 2 * tn_d, d, _epi_plain, [F32], tm, l, name="proj_out")
        x, xm = _ln_mod(x, y, mod_p, mod_s, 5, 1.0, alpha, ln_g4, ln_b4, l, 1, (mod_p, mod_s, 6), tr, seq)

        next_mod = (mods[l + 1][0], mods[l + 1][1], 0) if l + 1 < depth else None
        x, xm = ffn(x, xm, w_ffn2_in, w_ffn2_down, l, 8, 2, next_mod)

        outs["kp"].append(k[0].reshape(batch, seq, n_heads, head_dim))
        outs["vp"].append(v[0].reshape(batch, seq, n_heads, head_dim))
        outs["lfp"].append(lf[0][:, :n_heads].reshape(batch, seq, n_heads))
        outs["cvp"].append(u[0].reshape(batch, seq, ch)[:, seq - (width - 1):])
        outs["ks"].append(k[1].reshape(bs, t_new, n_heads, head_dim))
        outs["vs"].append(v[1].reshape(bs, t_new, n_heads, head_dim))
        outs["lfs"].append(lfn)
        outs["cvs"].append(jnp.concatenate([state_conv[l], u_s], axis=1)[:, -(width - 1):])

    return (x[0].reshape(batch, seq, d), x[1].reshape(bs, t_new, d),
            jnp.stack(outs["kp"]), jnp.stack(outs["vp"]), jnp.stack(outs["lfp"]), jnp.stack(outs["cvp"]),
            jnp.stack(outs["ks"]), jnp.stack(outs["vs"]), jnp.stack(outs["lfs"]), jnp.stack(outs["cvs"]))
```

```python
import functools

import jax
import jax.numpy as jnp
from jax import lax
from jax.experimental import pallas as pl
from jax.experimental.pallas import tpu as pltpu

F32 = jnp.float32
BF16 = jnp.bfloat16
LN_EPS = 1e-5
N_MOD = 9
LANES = 128
SUBLANES = 8
VMEM_LIMIT_BYTES = 60 * 1024 * 1024


def _params(*sem):
    return pltpu.CompilerParams(dimension_semantics=sem, vmem_limit_bytes=VMEM_LIMIT_BYTES)


def _split3(x):
    hi = x.astype(BF16)
    r1 = x - hi.astype(F32)
    mid = r1.astype(BF16)
    lo = (r1 - mid.astype(F32)).astype(BF16)
    return hi, mid, lo


def _dot01_left(m01, x):
    return sum(jnp.dot(m01, t, preferred_element_type=F32) for t in _split3(x))


def _dot01_right(x, m01):
    return sum(jnp.dot(t, m01, preferred_element_type=F32) for t in _split3(x))


def _adaln_kernel(c_ref, w_ref, b_ref, o_ref):
    c = c_ref[...]
    x = (c * jax.nn.sigmoid(c)).astype(BF16)
    o_ref[...] = jnp.dot(x, w_ref[...].astype(BF16), preferred_element_type=F32) + b_ref[...]


def _adaln(c_all, w_ada, b_ada3, layer, tn):
    rows, d = c_all.shape
    n = w_ada.shape[2]
    return pl.pallas_call(
        _adaln_kernel,
        grid=(n // tn,),
        in_specs=[pl.BlockSpec((rows, d), lambda j: (0, 0)),
                  pl.BlockSpec((None, d, tn), lambda j: (layer, 0, j)),
                  pl.BlockSpec((None, 1, tn), lambda j: (layer, 0, j))],
        out_specs=pl.BlockSpec((rows, tn), lambda j: (0, j)),
        out_shape=jax.ShapeDtypeStruct((rows, n), F32),
        compiler_params=_params("arbitrary"),
        name="adaln",
    )(c_all, w_ada, b_ada3)


def _epi_plain(accs, bias):
    return [jnp.concatenate(accs, axis=1) if len(accs) > 1 else accs[0]]


def _epi_swiglu(accs, bias):
    half = len(accs) // 2
    g = jnp.concatenate(accs[:half], axis=1)
    u = jnp.concatenate(accs[half:], axis=1)
    return [g * jax.nn.sigmoid(g) * u]


def _epi_glu(accs, bias):
    half = len(accs) // 2
    a = jnp.concatenate(accs[:half], axis=1)
    gl = jnp.concatenate(accs[half:], axis=1)
    return [a * jax.nn.sigmoid(gl)]


def _epi_logsig(accs, bias):
    z = accs[0] + bias
    return [jnp.minimum(z, 0.0) - jnp.log1p(jnp.exp(-jnp.abs(z)))]


def _mm_kernel(*refs, n_x, n_br, n_out, n_p, has_bias, has_add, w_t, epilogue):
    it = iter(refs)
    x_p = [next(it) for _ in range(n_x)]
    x_s = [next(it) for _ in range(n_x)]
    w = [[next(it) for _ in range(n_x)] for _ in range(n_br)]
    bias = next(it) if has_bias else None
    add_p = next(it) if has_add else None
    add_s = next(it) if has_add else None
    o_p = [next(it) for _ in range(n_out)]
    o_s = [next(it) for _ in range(n_out)]
    wb = [[next(it) for _ in range(n_x)] for _ in range(n_br)]
    i = pl.program_id(1)

    @pl.when(i == 0)
    def _():
        for b in range(n_br):
            for p in range(n_x):
                wb[b][p][...] = w[b][p][...].astype(BF16)

    def run(xs, add, outs):
        accs = []
        for b in range(n_br):
            acc = None
            for p in range(n_x):
                contract = (((1,), (1 if w_t else 0,)), ((), ()))
                d = lax.dot_general(xs[p][...], wb[b][p][...], contract, preferred_element_type=F32)
                acc = d if acc is None else acc + d
            accs.append(acc)
        if has_add:
            accs[0] = accs[0] + add[...]
        vals = epilogue(accs, bias[...] if has_bias else None)
        for o, v in zip(outs, vals):
            o[...] = v.astype(o.dtype)

    @pl.when(i < n_p)
    def _():
        run(x_p, add_p, o_p)

    @pl.when(i == n_p)
    def _():
        run(x_s, add_s, o_s)


def _mm(xs, w_specs, w_tn, out_tn, n_cols, epilogue, out_dtypes, tm, layer, bias=None, add=None, w_t=False,
        name="mm"):
    n_x, n_br, n_out = len(xs), len(w_specs), len(out_dtypes)
    m_p, m_s = xs[0][0][0].shape[0], xs[0][0][1].shape[0]
    n_p = m_p // tm
    grid = (pl.cdiv(n_cols, out_tn), n_p + 1)
    in_specs, args = [], []
    for (x_p, _), kp, kb in xs:
        in_specs.append(pl.BlockSpec((tm, kp), lambda j, i, kb=kb: (jnp.minimum(i, n_p - 1), kb)))
        args.append(x_p)
    for (_, x_s), kp, kb in xs:
        in_specs.append(pl.BlockSpec((m_s, kp), lambda j, i, kb=kb: (0, kb)))
        args.append(x_s)
    scratch = []
    for b in range(n_br):
        for p in range(n_x):
            w, rb, cm, co, cmax = w_specs[b][p]
            kp = xs[p][1]
            if w_t:
                in_specs.append(pl.BlockSpec(
                    (None, w_tn, kp),
                    lambda j, i, rb=rb, cm=cm, co=co, cmax=cmax: (layer, jnp.minimum(cm * j + co, cmax), rb)))
            else:
                in_specs.append(pl.BlockSpec(
                    (None, kp, w_tn),
                    lambda j, i, rb=rb, cm=cm, co=co, cmax=cmax: (layer, rb, jnp.minimum(cm * j + co, cmax))))
            args.append(w)
            scratch.append(pltpu.VMEM((w_tn, kp) if w_t else (kp, w_tn), BF16))
    if bias is not None:
        in_specs.append(pl.BlockSpec((None, 1, out_tn), lambda j, i: (layer, 0, j)))
        args.append(bias)
    out_p = pl.BlockSpec((tm, out_tn), lambda j, i: (jnp.minimum(i, n_p - 1), j))
    out_s = pl.BlockSpec((m_s, out_tn), lambda j, i: (0, j))
    if add is not None:
        in_specs += [out_p, out_s]
        args += [add[0], add[1]]
    out_specs = [out_p for _ in out_dtypes] + [out_s for _ in out_dtypes]
    out_shape = ([jax.ShapeDtypeStruct((m_p, n_cols), dt) for dt in out_dtypes]
                 + [jax.ShapeDtypeStruct((m_s, n_cols), dt) for dt in out_dtypes])
    outs = pl.pallas_call(
        functools.partial(_mm_kernel, n_x=n_x, n_br=n_br, n_out=n_out, n_p=n_p,
                          has_bias=bias is not None, has_add=add is not None, w_t=w_t, epilogue=epilogue),
        grid=grid, in_specs=in_specs, out_specs=out_specs, out_shape=out_shape,
        scratch_shapes=scratch,
        compiler_params=_params("arbitrary", "arbitrary"),
        name=name,
    )(*args)
    return [(outs[k], outs[n_out + k]) for k in range(n_out)]


def _w(w, row_block=0, col_off=0, col_mul=1, col_max=2 ** 30):
    return (w, row_block, col_mul, col_off, col_max)


def _shift_rows_kernel(a_ref, b_ref, o_ref, *, shift):
    o_ref[...] = jnp.concatenate([a_ref[shift:, :], b_ref[...]], axis=0)


def _shift_rows(w, start, n_rows, wb):
    depth, _, width = w.shape
    shift = start % wb
    base = start - shift
    assert shift > 0 and shift % SUBLANES == 0 and wb % shift == 0 and n_rows % wb == 0
    return pl.pallas_call(
        functools.partial(_shift_rows_kernel, shift=shift),
        grid=(depth, n_rows // wb),
        in_specs=[pl.BlockSpec((None, wb, width), lambda l, j: (l, base // wb + j, 0)),
                  pl.BlockSpec((None, shift, width), lambda l, j: (l, (base + (j + 1) * wb) // shift, 0))],
        out_specs=pl.BlockSpec((None, wb, width), lambda l, j: (l, j, 0)),
        out_shape=jax.ShapeDtypeStruct((depth, n_rows, width), w.dtype),
        compiler_params=_params("arbitrary", "arbitrary"),
        name="shift_rows",
    )(w, w)


def _mod_kernel(x_p, x_s, sh_p, sc_p, sh_s, sc_s, o_p, o_s, *, n_p):
    i = pl.program_id(0)

    @pl.when(i < n_p)
    def _():
        o_p[...] = (x_p[...] * (1.0 + sc_p[...]) + sh_p[...]).astype(o_p.dtype)

    @pl.when(i == n_p)
    def _():
        o_s[...] = (x_s[...] * (1.0 + sc_s[...]) + sh_s[...]).astype(o_s.dtype)


def _mod_specs(n_p, steps_per_seq, which, d, m_s):
    p = pl.BlockSpec((None, None, 1, d),
                     lambda i, w=which: (jnp.minimum(i, n_p - 1) // steps_per_seq, w, 0, 0))
    s = pl.BlockSpec((None, m_s, d), lambda i, w=which: (w, 0, 0))
    return p, s


def _modulate(x_p, x_s, mod_p, mod_s, which_shift, tr, seq):
    m_p, d = x_p.shape
    m_s = x_s.shape[0]
    n_p = m_p // tr
    spp = seq // tr
    sh_p, sh_s = _mod_specs(n_p, spp, which_shift, d, m_s)
    sc_p, sc_s = _mod_specs(n_p, spp, which_shift + 1, d, m_s)
    row_p = pl.BlockSpec((tr, d), lambda i: (jnp.minimum(i, n_p - 1), 0))
    row_s = pl.BlockSpec((m_s, d), lambda i: (0, 0))
    return pl.pallas_call(
        functools.partial(_mod_kernel, n_p=n_p),
        grid=(n_p + 1,),
        in_specs=[row_p, row_s, sh_p, sc_p, sh_s, sc_s],
        out_specs=[row_p, row_s],
        out_shape=[jax.ShapeDtypeStruct((m_p, d), BF16), jax.ShapeDtypeStruct((m_s, d), BF16)],
        compiler_params=_params("arbitrary"),
        name="modulate",
    )(x_p, x_s, mod_p, mod_p, mod_s, mod_s)


def _ln_kernel(*refs, n_p, alpha, coef, with_mod):
    if with_mod:
        (x_p, y_p, x_s, y_s, g_p, g_s, sh_p, sc_p, sh_s, sc_s, lng, lnb, o_p, o_s, m_p, m_s) = refs
    else:
        (x_p, y_p, x_s, y_s, g_p, g_s, lng, lnb, o_p, o_s) = refs
    i = pl.program_id(0)

    def run(x, y, gate, shift, scale, o, mo):
        z = alpha * x[...] + coef * gate[...] * y[...]
        mu = jnp.mean(z, axis=-1, keepdims=True)
        zc = z - mu
        var = jnp.mean(zc * zc, axis=-1, keepdims=True)
        out = zc * lax.rsqrt(var + LN_EPS) * lng[...] + lnb[...]
        o[...] = out
        if with_mod:
            mo[...] = (out * (1.0 + scale[...]) + shift[...]).astype(mo.dtype)

    @pl.when(i < n_p)
    def _():
        run(x_p, y_p, g_p, sh_p if with_mod else None, sc_p if with_mod else None, o_p,
            m_p if with_mod else None)

    @pl.when(i == n_p)
    def _():
        run(x_s, y_s, g_s, sh_s if with_mod else None, sc_s if with_mod else None, o_s,
            m_s if with_mod else None)


def _ln_mod(x, y, mod_p, mod_s, which_gate, coef, alpha, ln_g, ln_b, layer, which_ln, next_mod, tr, seq):
    (x_p, x_s), (y_p, y_s) = x, y
    m_p, d = x_p.shape
    m_s = x_s.shape[0]
    n_p = m_p // tr
    spp = seq // tr
    row_p = pl.BlockSpec((tr, d), lambda i: (jnp.minimum(i, n_p - 1), 0))
    row_s = pl.BlockSpec((m_s, d), lambda i: (0, 0))
    g_p, g_s = _mod_specs(n_p, spp, which_gate, d, m_s)
    ln_spec = pl.BlockSpec((None, None, 1, d), lambda i: (layer, which_ln, 0, 0))
    in_specs = [row_p, row_p, row_s, row_s, g_p, g_s]
    args = [x_p, y_p, x_s, y_s, mod_p, mod_s]
    out_specs = [row_p, row_s]
    out_shape = [jax.ShapeDtypeStruct((m_p, d), F32), jax.ShapeDtypeStruct((m_s, d), F32)]
    if next_mod is not None:
        nm_p, nm_s, ws = next_mod
        sh_p, sh_s = _mod_specs(n_p, spp, ws, d, m_s)
        sc_p, sc_s = _mod_specs(n_p, spp, ws + 1, d, m_s)
        in_specs += [sh_p, sc_p, sh_s, sc_s]
        args += [nm_p, nm_p, nm_s, nm_s]
        out_specs += [row_p, row_s]
        out_shape += [jax.ShapeDtypeStruct((m_p, d), BF16), jax.ShapeDtypeStruct((m_s, d), BF16)]
    in_specs += [ln_spec, ln_spec]
    args += [ln_g, ln_b]
    outs = pl.pallas_call(
        functools.partial(_ln_kernel, n_p=n_p, alpha=alpha, coef=coef, with_mod=next_mod is not None),
        grid=(n_p + 1,), in_specs=in_specs, out_specs=out_specs, out_shape=out_shape,
        compiler_params=_params("arbitrary"),
        name="ln_mod",
    )(*args)
    if next_mod is None:
        return (outs[0], outs[1]), None
    return (outs[0], outs[1]), (outs[2], outs[3])


def _pattn_kernel(q_ref, k_ref, v_ref, lf_ref, o_ref, kb, vb, cum, cum_t, *, tq, seq, scale):
    h = pl.program_id(1)
    qi = pl.program_id(2)
    n_blk = seq // tq

    @pl.when(qi == 0)
    def _():
        kb[...] = k_ref[...].astype(BF16)
        vb[...] = v_ref[...].astype(BF16)

    @pl.when((qi == 0) & (h == 0))
    def _():
        r = lax.broadcasted_iota(jnp.int32, (LANES, LANES), 0)
        c = lax.broadcasted_iota(jnp.int32, (LANES, LANES), 1)
        tri = (r >= c).astype(BF16)
        carry = jnp.zeros((1, LANES), F32)
        for blk in range(seq // LANES):
            x = lf_ref[blk * LANES:(blk + 1) * LANES, :]
            cb = _dot01_left(tri, x) + carry
            cum[blk * LANES:(blk + 1) * LANES, :] = cb
            carry = cb[LANES - 1:LANES, :]
        for blk in range(n_blk):
            cum_t[blk] = cum[blk * tq:(blk + 1) * tq, :].T

    q = q_ref[...]
    lane = lax.broadcasted_iota(jnp.int32, (tq, LANES), 1)
    cq = jnp.sum(jnp.where(lane == h, cum[pl.ds(pl.multiple_of(qi * tq, tq), tq), :], 0.0),
                 axis=1, keepdims=True)

    def block(ki, carry, masked):
        m, l, acc = carry
        start = pl.multiple_of(ki * tq, tq)
        s = lax.dot_general(q, kb[pl.ds(start, tq), :], (((1,), (1,)), ((), ())),
                            preferred_element_type=F32) * scale
        ck = cum_t[ki, pl.ds(h, 1), :]
        s = s - ck
        if masked:
            row = lax.broadcasted_iota(jnp.int32, (tq, tq), 0)
            col = lax.broadcasted_iota(jnp.int32, (tq, tq), 1)
            s = jnp.where(col <= row, s, -jnp.inf)
        m_new = jnp.maximum(m, jnp.max(s, axis=1, keepdims=True) + cq)
        a = jnp.exp(m - m_new)
        p = jnp.exp(s - (m_new - cq))
        l = a * l + jnp.sum(p, axis=1, keepdims=True)
        acc = a * acc + jnp.dot(p.astype(BF16), vb[pl.ds(start, tq), :], preferred_element_type=F32)
        return m_new, l, acc

    init = (jnp.full((tq, 1), -jnp.inf, F32), jnp.zeros((tq, 1), F32), jnp.zeros((tq, q_ref.shape[1]), F32))
    carry = lax.fori_loop(0, qi, lambda ki, cr: block(ki, cr, False), init)
    m, l, acc = block(qi, carry, True)
    o_ref[...] = (acc / l).astype(o_ref.dtype)


def _prompt_attention(q_p, k_p, v_p, lf_p, batch, seq, n_heads, head_dim, tq):
    m_p = q_p.shape[0]
    nq = seq // tq
    return pl.pallas_call(
        functools.partial(_pattn_kernel, tq=tq, seq=seq, scale=head_dim ** -0.5),
        grid=(batch, n_heads, nq),
        in_specs=[pl.BlockSpec((tq, head_dim), lambda b, h, qi: (b * nq + qi, h)),
                  pl.BlockSpec((seq, head_dim), lambda b, h, qi: (b, h)),
                  pl.BlockSpec((seq, head_dim), lambda b, h, qi: (b, h)),
                  pl.BlockSpec((seq, LANES), lambda b, h, qi: (b, 0))],
        out_specs=pl.BlockSpec((tq, head_dim), lambda b, h, qi: (b * nq + qi, h)),
        out_shape=jax.ShapeDtypeStruct((m_p, n_heads * head_dim), BF16),
        scratch_shapes=[pltpu.VMEM((seq, head_dim), BF16), pltpu.VMEM((seq, head_dim), BF16),
                        pltpu.VMEM((seq, LANES), F32), pltpu.VMEM((nq, LANES, tq), F32)],
        compiler_params=_params("arbitrary", "arbitrary", "arbitrary"),
        name="prompt_attention",
    )(q_p, k_p, v_p, lf_p)


def _sattn_kernel(pt_ref, wq_ref, kn_ref, vn_ref, lfn_ref, *refs, g_pages, page, n_heads, t_new, scale):
    k_refs = refs[:g_pages]
    v_refs = refs[g_pages:2 * g_pages]
    lf_refs = refs[2 * g_pages:3 * g_pages]
    o_ref = refs[3 * g_pages]
    kb, vb, lfbuf, rtbuf, crow_s, m_s, l_s, acc_s, carry_s = refs[3 * g_pages + 1:]
    st = pl.program_id(1)
    n_st = pl.num_programs(1)
    ht = n_heads * t_new
    hd = kb.shape[1]
    d = hd // n_heads
    wq = wq_ref[...]

    expand = (lax.broadcasted_iota(jnp.int32, (LANES, ht), 0)
              == lax.broadcasted_iota(jnp.int32, (LANES, ht), 1) // t_new).astype(BF16)
    key_h = lax.broadcasted_iota(jnp.int32, (n_heads, page), 1)
    key_q = lax.broadcasted_iota(jnp.int32, (page, ht), 0)
    lane_t = lax.broadcasted_iota(jnp.int32, (page, ht), 1) % t_new

    def update(s_t, v_chunk):
        m_old = m_s[0:1, :]
        m_new = jnp.maximum(m_old, jnp.max(s_t, axis=0, keepdims=True))
        a = jnp.exp(m_old - m_new)
        p = jnp.exp(s_t - m_new)
        l_s[...] = jnp.broadcast_to(a * l_s[0:1, :] + jnp.sum(p, axis=0, keepdims=True), l_s.shape)
        m_s[...] = jnp.broadcast_to(m_new, m_s.shape)
        o_full = jnp.dot(p.T.astype(BF16), v_chunk, preferred_element_type=F32)
        o_diag = jnp.concatenate(
            [o_full[hh * t_new:(hh + 1) * t_new, hh * d:(hh + 1) * d] for hh in range(n_heads)], axis=0)
        a_col = jnp.broadcast_to(a, (ht, ht)).T[:, 0:1]
        acc_s[...] = a_col * acc_s[...] + o_diag

    @pl.when(st == 0)
    def _():
        lfbuf[...] = jnp.zeros(lfbuf.shape, F32)
        rtbuf[...] = jnp.zeros(rtbuf.shape, F32)
        carry_s[...] = jnp.zeros(carry_s.shape, F32)
        m_s[...] = jnp.full(m_s.shape, -jnp.inf, F32)
        l_s[...] = jnp.zeros(l_s.shape, F32)
        acc_s[...] = jnp.zeros(acc_s.shape, F32)
        pad = jnp.zeros((page - t_new, hd), F32)
        kn = jnp.concatenate([kn_ref[...], pad], axis=0).astype(BF16)
        vn = jnp.concatenate([vn_ref[...], pad], axis=0).astype(BF16)
        lfbuf[0:t_new, 0:n_heads] = lfn_ref[...]
        tri = (lax.broadcasted_iota(jnp.int32, (page, page), 0)
               >= lax.broadcasted_iota(jnp.int32, (page, page), 1)).astype(BF16)
        c_nat = _dot01_left(tri, lfbuf[0:page, :])
        c_exp = _dot01_right(c_nat, expand)
        c_row = jnp.sum(jnp.where(key_q == lane_t, c_exp, 0.0), axis=0, keepdims=True)
        crow_s[...] = jnp.broadcast_to(c_row, crow_s.shape)
        s_t = jnp.dot(kn, wq, preferred_element_type=F32) * scale
        s_t = s_t + (c_row - c_exp)
        s_t = jnp.where(key_q <= lane_t, s_t, -jnp.inf)
        update(s_t, vn)

    c_row = crow_s[0:1, :]
    for g in range(g_pages):
        for hh in range(n_heads):
            kb[g * page:(g + 1) * page, hh * d:(hh + 1) * d] = (
                k_refs[g][pl.ds(hh, page, stride=n_heads), :].astype(BF16))
            vb[g * page:(g + 1) * page, hh * d:(hh + 1) * d] = (
                v_refs[g][pl.ds(hh, page, stride=n_heads), :].astype(BF16))
    carry = carry_s[...]
    for g in reversed(range(g_pages)):
        y = lf_refs[g][...]
        shift = 1
        while shift < page:
            y = y + jnp.where(key_h + shift < page, pltpu.roll(y, page - shift, axis=1), 0.0)
            shift *= 2
        excl = jnp.where(key_h + 1 < page, pltpu.roll(y, page - 1, axis=1), 0.0)
        rtbuf[g, 0:n_heads, :] = excl + carry
        carry = carry + jnp.broadcast_to(y[:, 0:1], carry.shape)
    carry_s[...] = carry
    r_nat = jnp.concatenate([rtbuf[g].T for g in range(g_pages)], axis=0)
    r_exp = _dot01_right(r_nat, expand)
    s_t = jnp.dot(kb[...], wq, preferred_element_type=F32) * scale
    s_t = s_t + (c_row + r_exp)
    update(s_t, vb[...])

    @pl.when(st == n_st - 1)
    def _():
        l_col = jnp.broadcast_to(l_s[0:1, :], (ht, ht)).T[:, 0:1]
        o_ref[...] = acc_s[...] / l_col


def _sample_attention(page_table, wq, kn, vn, lfn, cache_k, cache_v, cache_logf, layer, g_pages,
                      n_heads, head_dim, t_new):
    bs, n_pages = page_table.shape
    page = cache_k.shape[2]
    hd = n_heads * head_dim
    ht = n_heads * t_new
    n_st = n_pages // g_pages
    ck = cache_k.reshape(cache_k.shape[0], cache_k.shape[1], page * n_heads, head_dim)
    cv = cache_v.reshape(cache_v.shape[0], cache_v.shape[1], page * n_heads, head_dim)
    clf = cache_logf.transpose(0, 1, 3, 2)

    def page_map(g):
        def f(b, st, pt):
            return (layer, pt[b * n_pages + (n_st - 1 - st) * g_pages + g], 0, 0)
        return f

    in_specs = [pl.BlockSpec((None, hd, ht), lambda b, st, pt: (b, 0, 0)),
                pl.BlockSpec((None, t_new, hd), lambda b, st, pt: (b, 0, 0)),
                pl.BlockSpec((None, t_new, hd), lambda b, st, pt: (b, 0, 0)),
                pl.BlockSpec((None, t_new, n_heads), lambda b, st, pt: (b, 0, 0))]
    in_specs += [pl.BlockSpec((None, None, page * n_heads, head_dim), page_map(g)) for g in range(g_pages)]
    in_specs += [pl.BlockSpec((None, None, page * n_heads, head_dim), page_map(g)) for g in range(g_pages)]
    in_specs += [pl.BlockSpec((None, None, n_heads, page), page_map(g)) for g in range(g_pages)]
    grid_spec = pltpu.PrefetchScalarGridSpec(
        num_scalar_prefetch=1, grid=(bs, n_st), in_specs=in_specs,
        out_specs=pl.BlockSpec((None, ht, head_dim), lambda b, st, pt: (b, 0, 0)),
        scratch_shapes=[pltpu.VMEM((g_pages * page, hd), BF16), pltpu.VMEM((g_pages * page, hd), BF16),
                        pltpu.VMEM((page, LANES), F32), pltpu.VMEM((g_pages, LANES, page), F32),
                        pltpu.VMEM((8, ht), F32), pltpu.VMEM((8, ht), F32), pltpu.VMEM((8, ht), F32),
                        pltpu.VMEM((ht, head_dim), F32), pltpu.VMEM((n_heads, page), F32)])
    return pl.pallas_call(
        functools.partial(_sattn_kernel, g_pages=g_pages, page=page, n_heads=n_heads, t_new=t_new,
                          scale=head_dim ** -0.5),
        grid_spec=grid_spec,
        out_shape=jax.ShapeDtypeStruct((bs, ht, head_dim), F32),
        compiler_params=_params("arbitrary", "arbitrary"),
        name="sample_attention",
    )(page_table.reshape(-1), wq, kn, vn, lfn, *([ck] * g_pages), *([cv] * g_pages),
      *([clf] * g_pages))


def _conv_rows(window, w_ref, bdw, gcn, bcn, width, group):
    ch = w_ref.shape[1]
    parts = []
    for c0 in range(0, ch, group):
        acc = None
        for j in range(width):
            term = w_ref[j:j + 1, c0:c0 + group] * window(j, c0, c0 + group)
            acc = term if acc is None else acc + term
        parts.append(acc)
    y = jnp.concatenate(parts, axis=1) + bdw
    mu = jnp.mean(y, axis=-1, keepdims=True)
    yc = y - mu
    var = jnp.mean(yc * yc, axis=-1, keepdims=True)
    z = yc * lax.rsqrt(var + LN_EPS) * gcn + bcn
    return z * jax.nn.sigmoid(z)


def _pconv_kernel(u_ref, halo_ref, w_ref, bdw_ref, gcn_ref, bcn_ref, o_ref, ext, *, tt, halo, width, chunk,
                  group):
    ti = pl.program_id(1)
    n_ext = halo + tt
    ext[0, 0:halo, :] = jnp.where(ti > 0, halo_ref[...], 0.0)
    ext[0, halo:n_ext, :] = u_ref[...]
    for r in range(1, SUBLANES):
        ext[r, 0:n_ext - SUBLANES, :] = ext[0, r:r + n_ext - SUBLANES, :]
    bdw, gcn, bcn = bdw_ref[...], gcn_ref[...], bcn_ref[...]
    first = halo - (width - 1)

    def body(c, _):
        r0 = pl.multiple_of(c * chunk, chunk)

        def window(j, c0, c1):
            off = first + j
            return ext[off % SUBLANES, pl.ds(r0 + (off // SUBLANES) * SUBLANES, chunk), c0:c1]

        z = _conv_rows(window, w_ref, bdw, gcn, bcn, width, group)
        o_ref[pl.ds(r0, chunk), :] = z.astype(o_ref.dtype)
        return 0

    lax.fori_loop(0, tt // chunk, body, 0)


def _prompt_conv(u_p, w_dw, b_dw, g_cn, b_cn, layer, batch, seq, tt, chunk, group):
    m_p, ch = u_p.shape
    width = w_dw.shape[1]
    halo = 32
    assert width - 1 <= halo and tt % halo == 0
    nt = seq // tt
    vec = pl.BlockSpec((None, 1, ch), lambda b, ti: (layer, 0, 0))
    return pl.pallas_call(
        functools.partial(_pconv_kernel, tt=tt, halo=halo, width=width, chunk=chunk, group=group),
        grid=(batch, nt),
        in_specs=[pl.BlockSpec((tt, ch), lambda b, ti: (b * nt + ti, 0)),
                  pl.BlockSpec((halo, ch), lambda b, ti: (jnp.maximum((b * nt + ti) * (tt // halo) - 1, 0), 0)),
                  pl.BlockSpec((None, width, ch), lambda b, ti: (layer, 0, 0)),
                  vec, vec, vec],
        out_specs=pl.BlockSpec((tt, ch), lambda b, ti: (b * nt + ti, 0)),
        out_shape=jax.ShapeDtypeStruct((m_p, ch), BF16),
        scratch_shapes=[pltpu.VMEM((SUBLANES, halo + tt, ch), F32)],
        compiler_params=_params("arbitrary", "arbitrary"),
        name="prompt_conv",
    )(u_p, u_p, w_dw, b_dw, g_cn, b_cn)


def _sconv_kernel(ext_ref, w_ref, bdw_ref, gcn_ref, bcn_ref, o_ref, *, n_seq, t_new, width):
    bdw, gcn, bcn = bdw_ref[...], gcn_ref[...], bcn_ref[...]
    for b in range(n_seq):
        z = _conv_rows(lambda j, c0, c1, b=b: ext_ref[b, j:j + t_new, c0:c1], w_ref, bdw, gcn, bcn, width,
                       w_ref.shape[1])
        o_ref[b * t_new:(b + 1) * t_new, :] = z


def _sample_conv(ext_s, w_dw, b_dw, g_cn, b_cn, layer, t_new):
    n_seq, rows, ch = ext_s.shape
    width = w_dw.shape[1]
    vec = pl.BlockSpec((None, 1, ch), lambda i: (layer, 0, 0))
    return pl.pallas_call(
        functools.partial(_sconv_kernel, n_seq=n_seq, t_new=t_new, width=width),
        grid=(1,),
        in_specs=[pl.BlockSpec((n_seq, rows, ch), lambda i: (0, 0, 0)),
                  pl.BlockSpec((None, width, ch), lambda i: (layer, 0, 0)),
                  vec, vec, vec],
        out_specs=pl.BlockSpec((n_seq * t_new, ch), lambda i: (0, 0)),
        out_shape=jax.ShapeDtypeStruct((n_seq * t_new, ch), F32),
        compiler_params=_params("arbitrary"),
        name="sample_conv",
    )(ext_s, w_dw, b_dw, g_cn, b_cn)


def _largest_divisor(n, cap, multiple):
    best = None
    for t in range(multiple, min(n, cap) + 1, multiple):
        if n % t == 0:
            best = t
    assert best is not None, (n, cap, multiple)
    return best


def kernel(x_prompt, x_sample, c_prompt, c_sample, cache_k, cache_v, cache_logf, state_conv, page_table,
           w_ada, b_ada, w_ffn1_in, w_ffn1_down, w_in, b_f, w_dw, b_dw, g_cn, b_cn, w_out,
           w_ffn2_in, w_ffn2_down, ln_g, ln_b):
    batch, seq, d = x_prompt.shape
    bs, t_new, _ = x_sample.shape
    depth = w_ada.shape[0]
    n_heads, head_dim = cache_k.shape[3], cache_k.shape[4]
    aw = n_heads * head_dim
    ch = d - aw
    d_ff = w_ffn1_down.shape[1]
    width = w_dw.shape[1]
    alpha = (2 * depth) ** 0.25
    m_p, m_s = batch * seq, bs * t_new

    tm = _largest_divisor(m_p, 512, 16)
    tr = _largest_divisor(seq, 256, 16)
    tn_ff = _largest_divisor(d_ff, 256, LANES)
    tn_aw = _largest_divisor(aw, 256, LANES)
    tn_ch = _largest_divisor(ch, 256, LANES)
    tn_d = _largest_divisor(d, 512, LANES)
    nb_ff, nb_aw, nb_ch = d_ff // tn_ff, aw // tn_aw, ch // tn_ch
    k_split = 2 if d_ff % (2 * LANES) == 0 else 1
    tn_ada = _largest_divisor(N_MOD * d, 512, LANES)
    tq = _largest_divisor(seq, 1024, LANES)
    tt = _largest_divisor(seq, 256, 32)
    conv_chunk = _largest_divisor(tt, 64, 16)
    conv_group = _largest_divisor(ch, 128, LANES)
    g_pages = _largest_divisor(page_table.shape[1], 8, 1)

    n_seq_all = batch + bs
    c_all = jnp.concatenate([c_prompt, c_sample, jnp.zeros((-n_seq_all % 8, d), F32)], axis=0)
    b_ada3 = b_ada.reshape(depth, 1, N_MOD * d)
    mods = []
    for l in range(depth):
        m = _adaln(c_all, w_ada, b_ada3, l, tn_ada)
        mod_p = m[:batch].reshape(batch, N_MOD, 1, d)
        mod_s = jnp.repeat(m[batch:n_seq_all].reshape(bs, N_MOD, d).transpose(1, 0, 2), t_new, axis=1)
        mods.append((mod_p, mod_s))

    c_f = 3 * aw
    c_a = c_f + n_heads
    c_gl = c_a + ch
    assert c_f % LANES == 0 and n_heads < LANES
    w_in_t = w_in.transpose(0, 2, 1)
    b_f3 = jnp.pad(b_f, ((0, 0), (0, LANES - n_heads))).reshape(depth, 1, LANES)
    w_a_t = _shift_rows(w_in_t, c_a, ch, tn_ch)
    w_gl_t = _shift_rows(w_in_t, c_gl, ch, tn_ch)
    ln_g4 = ln_g.reshape(depth, 3, 1, d)
    ln_b4 = ln_b.reshape(depth, 3, 1, d)
    b_dw3, g_cn3, b_cn3 = (v.reshape(depth, 1, ch) for v in (b_dw, g_cn, b_cn))
    eye_h = jnp.eye(n_heads, dtype=bool)

    x = (x_prompt.reshape(m_p, d), x_sample.reshape(m_s, d))
    xm = _modulate(x[0], x[1], mods[0][0], mods[0][1], 0, tr, seq)

    def cols(w, first, n_blocks, pieces):
        return [[_w(w, col_off=first + q, col_mul=pieces, col_max=first + n_blocks - 1)] for q in range(pieces)]

    def ffn(x, xm, w_in_l, w_down_l, l, which_gate, which_ln, next_mod):
        (h,) = _mm([(xm, d, 0)], cols(w_in_l, 0, nb_ff, 2) + cols(w_in_l, nb_ff, nb_ff, 2), tn_ff, 2 * tn_ff,
                   d_ff, _epi_swiglu, [BF16], tm, l, name="ffn_in")
        y = None
        for kb in range(k_split):
            (y,) = _mm([(h, d_ff // k_split, kb)], [[_w(w_down_l, row_block=kb)]], tn_d, tn_d, d, _epi_plain,
                       [F32], tm, l, add=y, name="ffn_down")
        return _ln_mod(x, y, mods[l][0], mods[l][1], which_gate, 0.5, alpha, ln_g4, ln_b4, l, which_ln,
                       next_mod, tr, seq)

    outs = {k: [] for k in ("kp", "vp", "lfp", "cvp", "ks", "vs", "lfs", "cvs")}
    for l in range(depth):
        mod_p, mod_s = mods[l]
        x, xm = ffn(x, xm, w_ffn1_in, w_ffn1_down, l, 2, 0, (mod_p, mod_s, 3))

        xin = [(xm, d, 0)]
        (q,) = _mm(xin, cols(w_in_t, 0, nb_aw, 4), tn_aw, 4 * tn_aw, aw, _epi_plain, [BF16], tm, l, w_t=True,
                   name="proj_q")
        (k,) = _mm(xin, cols(w_in_t, nb_aw, nb_aw, 4), tn_aw, 4 * tn_aw, aw, _epi_plain, [F32], tm, l, w_t=True,
                   name="proj_k")
        (v,) = _mm(xin, cols(w_in_t, 2 * nb_aw, nb_aw, 4), tn_aw, 4 * tn_aw, aw, _epi_plain, [F32], tm, l,
                   w_t=True, name="proj_v")
        (lf,) = _mm(xin, [[_w(w_in_t, col_off=c_f // LANES)]], LANES, LANES, LANES, _epi_logsig, [F32], tm, l,
                    bias=b_f3, w_t=True, name="proj_f")
        (u,) = _mm(xin, cols(w_a_t, 0, nb_ch, 2) + cols(w_gl_t, 0, nb_ch, 2), tn_ch, 2 * tn_ch, ch, _epi_glu,
                   [F32], tm, l, w_t=True, name="proj_glu")

        attn_p = _prompt_attention(q[0], k[0], v[0], lf[0], batch, seq, n_heads, head_dim, tq)
        conv_p = _prompt_conv(u[0], w_dw, b_dw3, g_cn3, b_cn3, l, batch, seq, tt, conv_chunk, conv_group)

        q_s = q[1].reshape(bs, t_new, n_heads, head_dim).transpose(0, 2, 3, 1)
        wq = jnp.where(eye_h[None, :, None, :, None], q_s[:, :, :, None, :], 0).reshape(
            bs, aw, n_heads * t_new)
        lfn = lf[1][:, :n_heads].reshape(bs, t_new, n_heads)
        o_s = _sample_attention(page_table, wq, k[1].reshape(bs, t_new, aw), v[1].reshape(bs, t_new, aw),
                                lfn, cache_k, cache_v, cache_logf, l, g_pages, n_heads, head_dim, t_new)
        attn_s = o_s.reshape(bs, n_heads, t_new, head_dim).transpose(0, 2, 1, 3).reshape(m_s, aw).astype(BF16)
        u_s = u[1].reshape(bs, t_new, ch)
        ext_s = jnp.concatenate([state_conv[l], u_s, jnp.zeros((bs, -(width - 1 + t_new) % 8, ch), F32)],
                                axis=1)
        conv_s = _sample_conv(ext_s, w_dw, b_dw3, g_cn3, b_cn3, l, t_new).astype(BF16)

        assert aw == ch
        nb_d = d // tn_d
        (y,) = _mm([((attn_p, attn_s), aw, 0), ((conv_p, conv_s), ch, 0)],
                   [[_w(w_out, row_block=0, col_off=q, col_mul=2, col_max=nb_d - 1),
                     _w(w_out, row_block=1, col_off=q, col_mul=2, col_max=nb_d - 1)] for q in range(2)],
                   tn_d, 2 * tn_d, d, _epi_plain, [F32], tm, l, name="proj_out")
        x, xm = _ln_mod(x, y, mod_p, mod_s, 5, 1.0, alpha, ln_g4, ln_b4, l, 1, (mod_p, mod_s, 6), tr, seq)

        next_mod = (mods[l + 1][0], mods[l + 1][1], 0) if l + 1 < depth else None
        x, xm = ffn(x, xm, w_ffn2_in, w_ffn2_down, l, 8, 2, next_mod)

        outs["kp"].append(k[0].reshape(batch, seq, n_heads, head_dim))
        outs["vp"].append(v[0].reshape(batch, seq, n_heads, head_dim))
        outs["lfp"].append(lf[0][:, :n_heads].reshape(batch, seq, n_heads))
        outs["cvp"].append(u[0].reshape(batch, seq, ch)[:, seq - (width - 1):])
        outs["ks"].append(k[1].reshape(bs, t_new, n_heads, head_dim))
        outs["vs"].append(v[1].reshape(bs, t_new, n_heads, head_dim))
        outs["lfs"].append(lfn)
        outs["cvs"].append(jnp.concatenate([state_conv[l], u_s], axis=1)[:, -(width - 1):])

    return (x[0].reshape(batch, seq, d), x[1].reshape(bs, t_new, d),
            jnp.stack(outs["kp"]), jnp.stack(outs["vp"]), jnp.stack(outs["lfp"]), jnp.stack(outs["cvp"]),
            jnp.stack(outs["ks"]), jnp.stack(outs["vs"]), jnp.stack(outs["lfs"]), jnp.stack(outs["cvs"]))
```

```python
import functools

import jax
import jax.numpy as jnp
from jax import lax
from jax.experimental import pallas as pl
from jax.experimental.pallas import tpu as pltpu

F32 = jnp.float32
BF16 = jnp.bfloat16
LN_EPS = 1e-5
N_MOD = 9
LANES = 128
SUBLANES = 8
VMEM_LIMIT_BYTES = 60 * 1024 * 1024


def _params(*sem):
    return pltpu.CompilerParams(dimension_semantics=sem, vmem_limit_bytes=VMEM_LIMIT_BYTES)


def _split3(x):
    hi = x.astype(BF16)
    r1 = x - hi.astype(F32)
    mid = r1.astype(BF16)
    lo = (r1 - mid.astype(F32)).astype(BF16)
    return hi, mid, lo


def _dot01_left(m01, x):
    return sum(jnp.dot(m01, t, preferred_element_type=F32) for t in _split3(x))


def _dot01_right(x, m01):
    return sum(jnp.dot(t, m01, preferred_element_type=F32) for t in _split3(x))


def _adaln_kernel(c_ref, w_ref, b_ref, o_ref):
    c = c_ref[...]
    x = (c * jax.nn.sigmoid(c)).astype(BF16)
    o_ref[...] = jnp.dot(x, w_ref[...].astype(BF16), preferred_element_type=F32) + b_ref[...]


def _adaln(c_all, w_ada, b_ada3, layer, tn):
    rows, d = c_all.shape
    n = w_ada.shape[2]
    return pl.pallas_call(
        _adaln_kernel,
        grid=(n // tn,),
        in_specs=[pl.BlockSpec((rows, d), lambda j: (0, 0)),
                  pl.BlockSpec((None, d, tn), lambda j: (layer, 0, j)),
                  pl.BlockSpec((None, 1, tn), lambda j: (layer, 0, j))],
        out_specs=pl.BlockSpec((rows, tn), lambda j: (0, j)),
        out_shape=jax.ShapeDtypeStruct((rows, n), F32),
        compiler_params=_params("arbitrary"),
        name="adaln",
    )(c_all, w_ada, b_ada3)


def _epi_plain(accs, bias):
    return [jnp.concatenate(accs, axis=1) if len(accs) > 1 else accs[0]]


def _epi_swiglu(accs, bias):
    half = len(accs) // 2
    g = jnp.concatenate(accs[:half], axis=1)
    u = jnp.concatenate(accs[half:], axis=1)
    return [g * jax.nn.sigmoid(g) * u]


def _epi_glu(accs, bias):
    half = len(accs) // 2
    a = jnp.concatenate(accs[:half], axis=1)
    gl = jnp.concatenate(accs[half:], axis=1)
    return [a * jax.nn.sigmoid(gl)]


def _epi_logsig(accs, bias):
    z = accs[0] + bias
    return [jnp.minimum(z, 0.0) - jnp.log1p(jnp.exp(-jnp.abs(z)))]


def _mm_kernel(*refs, n_x, n_br, n_out, n_p, has_bias, has_add, w_t, epilogue):
    it = iter(refs)
    x_p = [next(it) for _ in range(n_x)]
    x_s = [next(it) for _ in range(n_x)]
    w = [[next(it) for _ in range(n_x)] for _ in range(n_br)]
    bias = next(it) if has_bias else None
    add_p = next(it) if has_add else None
    add_s = next(it) if has_add else None
    o_p = [next(it) for _ in range(n_out)]
    o_s = [next(it) for _ in range(n_out)]
    wb = [[next(it) for _ in range(n_x)] for _ in range(n_br)]
    i = pl.program_id(1)

    @pl.when(i == 0)
    def _():
        for b in range(n_br):
            for p in range(n_x):
                wb[b][p][...] = w[b][p][...].astype(BF16)

    def run(xs, add, outs):
        accs = []
        for b in range(n_br):
            acc = None
            for p in range(n_x):
                contract = (((1,), (1 if w_t else 0,)), ((), ()))
                d = lax.dot_general(xs[p][...], wb[b][p][...], contract, preferred_element_type=F32)
                acc = d if acc is None else acc + d
            accs.append(acc)
        if has_add:
            accs[0] = accs[0] + add[...]
        vals = epilogue(accs, bias[...] if has_bias else None)
        for o, v in zip(outs, vals):
            o[...] = v.astype(o.dtype)

    @pl.when(i < n_p)
    def _():
        run(x_p, add_p, o_p)

    @pl.when(i == n_p)
    def _():
        run(x_s, add_s, o_s)


def _mm(xs, w_specs, w_tn, out_tn, n_cols, epilogue, out_dtypes, tm, layer, bias=None, add=None, w_t=False,
        name="mm"):
    n_x, n_br, n_out = len(xs), len(w_specs), len(out_dtypes)
    m_p, m_s = xs[0][0][0].shape[0], xs[0][0][1].shape[0]
    n_p = m_p // tm
    grid = (pl.cdiv(n_cols, out_tn), n_p + 1)
    in_specs, args = [], []
    for (x_p, _), kp, kb in xs:
        in_specs.append(pl.BlockSpec((tm, kp), lambda j, i, kb=kb: (jnp.minimum(i, n_p - 1), kb)))
        args.append(x_p)
    for (_, x_s), kp, kb in xs:
        in_specs.append(pl.BlockSpec((m_s, kp), lambda j, i, kb=kb: (0, kb)))
        args.append(x_s)
    scratch = []
    for b in range(n_br):
        for p in range(n_x):
            w, rb, cm, co, cmax = w_specs[b][p]
            kp = xs[p][1]
            if w_t:
                in_specs.append(pl.BlockSpec(
                    (None, w_tn, kp),
                    lambda j, i, rb=rb, cm=cm, co=co, cmax=cmax: (layer, jnp.minimum(cm * j + co, cmax), rb)))
            else:
                in_specs.append(pl.BlockSpec(
                    (None, kp, w_tn),
                    lambda j, i, rb=rb, cm=cm, co=co, cmax=cmax: (layer, rb, jnp.minimum(cm * j + co, cmax))))
            args.append(w)
            scratch.append(pltpu.VMEM((w_tn, kp) if w_t else (kp, w_tn), BF16))
    if bias is not None:
        in_specs.append(pl.BlockSpec((None, 1, out_tn), lambda j, i: (layer, 0, j)))
        args.append(bias)
    out_p = pl.BlockSpec((tm, out_tn), lambda j, i: (jnp.minimum(i, n_p - 1), j))
    out_s = pl.BlockSpec((m_s, out_tn), lambda j, i: (0, j))
    if add is not None:
        in_specs += [out_p, out_s]
        args += [add[0], add[1]]
    out_specs = [out_p for _ in out_dtypes] + [out_s for _ in out_dtypes]
    out_shape = ([jax.ShapeDtypeStruct((m_p, n_cols), dt) for dt in out_dtypes]
                 + [jax.ShapeDtypeStruct((m_s, n_cols), dt) for dt in out_dtypes])
    outs = pl.pallas_call(
        functools.partial(_mm_kernel, n_x=n_x, n_br=n_br, n_out=n_out, n_p=n_p,
                          has_bias=bias is not None, has_add=add is not None, w_t=w_t, epilogue=epilogue),
        grid=grid, in_specs=in_specs, out_specs=out_specs, out_shape=out_shape,
        scratch_shapes=scratch,
        compiler_params=_params("arbitrary", "arbitrary"),
        name=name,
    )(*args)
    return [(outs[k], outs[n_out + k]) for k in range(n_out)]


def _w(w, row_block=0, col_off=0, col_mul=1, col_max=2 ** 30):
    return (w, row_block, col_mul, col_off, col_max)


def _shift_rows_kernel(a_ref, b_ref, o_ref, *, shift):
    o_ref[...] = jnp.concatenate([a_ref[shift:, :], b_ref[...]], axis=0)


def _shift_rows(w, start, n_rows, wb):
    depth, _, width = w.shape
    shift = start % wb
    base = start - shift
    assert shift > 0 and shift % SUBLANES == 0 and wb % shift == 0 and n_rows % wb == 0
    return pl.pallas_call(
        functools.partial(_shift_rows_kernel, shift=shift),
        grid=(depth, n_rows // wb),
        in_specs=[pl.BlockSpec((None, wb, width), lambda l, j: (l, base // wb + j, 0)),
                  pl.BlockSpec((None, shift, width), lambda l, j: (l, (base + (j + 1) * wb) // shift, 0))],
        out_specs=pl.BlockSpec((None, wb, width), lambda l, j: (l, j, 0)),
        out_shape=jax.ShapeDtypeStruct((depth, n_rows, width), w.dtype),
        compiler_params=_params("arbitrary", "arbitrary"),
        name="shift_rows",
    )(w, w)


def _mod_kernel(x_p, x_s, sh_p, sc_p, sh_s, sc_s, o_p, o_s, *, n_p):
    i = pl.program_id(0)

    @pl.when(i < n_p)
    def _():
        o_p[...] = (x_p[...] * (1.0 + sc_p[...]) + sh_p[...]).astype(o_p.dtype)

    @pl.when(i == n_p)
    def _():
        o_s[...] = (x_s[...] * (1.0 + sc_s[...]) + sh_s[...]).astype(o_s.dtype)


def _mod_specs(n_p, steps_per_seq, which, d, m_s):
    p = pl.BlockSpec((None, None, 1, d),
                     lambda i, w=which: (jnp.minimum(i, n_p - 1) // steps_per_seq, w, 0, 0))
    s = pl.BlockSpec((None, m_s, d), lambda i, w=which: (w, 0, 0))
    return p, s


def _modulate(x_p, x_s, mod_p, mod_s, which_shift, tr, seq):
    m_p, d = x_p.shape
    m_s = x_s.shape[0]
    n_p = m_p // tr
    spp = seq // tr
    sh_p, sh_s = _mod_specs(n_p, spp, which_shift, d, m_s)
    sc_p, sc_s = _mod_specs(n_p, spp, which_shift + 1, d, m_s)
    row_p = pl.BlockSpec((tr, d), lambda i: (jnp.minimum(i, n_p - 1), 0))
    row_s = pl.BlockSpec((m_s, d), lambda i: (0, 0))
    return pl.pallas_call(
        functools.partial(_mod_kernel, n_p=n_p),
        grid=(n_p + 1,),
        in_specs=[row_p, row_s, sh_p, sc_p, sh_s, sc_s],
        out_specs=[row_p, row_s],
        out_shape=[jax.ShapeDtypeStruct((m_p, d), BF16), jax.ShapeDtypeStruct((m_s, d), BF16)],
        compiler_params=_params("arbitrary"),
        name="modulate",
    )(x_p, x_s, mod_p, mod_p, mod_s, mod_s)


def _ln_kernel(*refs, n_p, alpha, coef, with_mod):
    if with_mod:
        (x_p, y_p, x_s, y_s, g_p, g_s, sh_p, sc_p, sh_s, sc_s, lng, lnb, o_p, o_s, m_p, m_s) = refs
    else:
        (x_p, y_p, x_s, y_s, g_p, g_s, lng, lnb, o_p, o_s) = refs
    i = pl.program_id(0)

    def run(x, y, gate, shift, scale, o, mo):
        z = alpha * x[...] + coef * gate[...] * y[...].astype(F32)
        mu = jnp.mean(z, axis=-1, keepdims=True)
        zc = z - mu
        var = jnp.mean(zc * zc, axis=-1, keepdims=True)
        out = zc * lax.rsqrt(var + LN_EPS) * lng[...] + lnb[...]
        o[...] = out
        if with_mod:
            mo[...] = (out * (1.0 + scale[...]) + shift[...]).astype(mo.dtype)

    @pl.when(i < n_p)
    def _():
        run(x_p, y_p, g_p, sh_p if with_mod else None, sc_p if with_mod else None, o_p,
            m_p if with_mod else None)

    @pl.when(i == n_p)
    def _():
        run(x_s, y_s, g_s, sh_s if with_mod else None, sc_s if with_mod else None, o_s,
            m_s if with_mod else None)


def _ln_mod(x, y, mod_p, mod_s, which_gate, coef, alpha, ln_g, ln_b, layer, which_ln, next_mod, tr, seq):
    (x_p, x_s), (y_p, y_s) = x, y
    m_p, d = x_p.shape
    m_s = x_s.shape[0]
    n_p = m_p // tr
    spp = seq // tr
    row_p = pl.BlockSpec((tr, d), lambda i: (jnp.minimum(i, n_p - 1), 0))
    row_s = pl.BlockSpec((m_s, d), lambda i: (0, 0))
    g_p, g_s = _mod_specs(n_p, spp, which_gate, d, m_s)
    ln_spec = pl.BlockSpec((None, None, 1, d), lambda i: (layer, which_ln, 0, 0))
    in_specs = [row_p, row_p, row_s, row_s, g_p, g_s]
    args = [x_p, y_p, x_s, y_s, mod_p, mod_s]
    out_specs = [row_p, row_s]
    out_shape = [jax.ShapeDtypeStruct((m_p, d), F32), jax.ShapeDtypeStruct((m_s, d), F32)]
    if next_mod is not None:
        nm_p, nm_s, ws = next_mod
        sh_p, sh_s = _mod_specs(n_p, spp, ws, d, m_s)
        sc_p, sc_s = _mod_specs(n_p, spp, ws + 1, d, m_s)
        in_specs += [sh_p, sc_p, sh_s, sc_s]
        args += [nm_p, nm_p, nm_s, nm_s]
        out_specs += [row_p, row_s]
        out_shape += [jax.ShapeDtypeStruct((m_p, d), BF16), jax.ShapeDtypeStruct((m_s, d), BF16)]
    in_specs += [ln_spec, ln_spec]
    args += [ln_g, ln_b]
    outs = pl.pallas_call(
        functools.partial(_ln_kernel, n_p=n_p, alpha=alpha, coef=coef, with_mod=next_mod is not None),
        grid=(n_p + 1,), in_specs=in_specs, out_specs=out_specs, out_shape=out_shape,
        compiler_params=_params("arbitrary"),
        name="ln_mod",
    )(*args)
    if next_mod is None:
        return (outs[0], outs[1]), None
    return (outs[0], outs[1]), (outs[2], outs[3])


def _pattn_kernel(q_ref, k_ref, v_ref, lf_ref, o_ref, kb, vb, cum, cum_t, *, tq, seq, scale):
    h = pl.program_id(1)
    qi = pl.program_id(2)
    n_blk = seq // tq

    @pl.when(qi == 0)
    def _():
        kb[...] = k_ref[...].astype(BF16)
        vb[...] = v_ref[...].astype(BF16)

    @pl.when((qi == 0) & (h == 0))
    def _():
        r = lax.broadcasted_iota(jnp.int32, (LANES, LANES), 0)
        c = lax.broadcasted_iota(jnp.int32, (LANES, LANES), 1)
        tri = (r >= c).astype(BF16)
        carry = jnp.zeros((1, LANES), F32)
        for blk in range(seq // LANES):
            x = lf_ref[blk * LANES:(blk + 1) * LANES, :]
            cb = _dot01_left(tri, x) + carry
            cum[blk * LANES:(blk + 1) * LANES, :] = cb
            carry = cb[LANES - 1:LANES, :]
        for blk in range(n_blk):
            cum_t[blk] = cum[blk * tq:(blk + 1) * tq, :].T

    q = q_ref[...]
    lane = lax.broadcasted_iota(jnp.int32, (tq, LANES), 1)
    cq = jnp.sum(jnp.where(lane == h, cum[pl.ds(pl.multiple_of(qi * tq, tq), tq), :], 0.0),
                 axis=1, keepdims=True)

    def block(ki, carry, masked):
        m, l, acc = carry
        start = pl.multiple_of(ki * tq, tq)
        s = lax.dot_general(q, kb[pl.ds(start, tq), :], (((1,), (1,)), ((), ())),
                            preferred_element_type=F32) * scale
        ck = cum_t[ki, pl.ds(h, 1), :]
        s = s - ck
        if masked:
            row = lax.broadcasted_iota(jnp.int32, (tq, tq), 0)
            col = lax.broadcasted_iota(jnp.int32, (tq, tq), 1)
            s = jnp.where(col <= row, s, -jnp.inf)
        m_new = jnp.maximum(m, jnp.max(s, axis=1, keepdims=True) + cq)
        a = jnp.exp(m - m_new)
        p = jnp.exp(s - (m_new - cq))
        l = a * l + jnp.sum(p, axis=1, keepdims=True)
        acc = a * acc + jnp.dot(p.astype(BF16), vb[pl.ds(start, tq), :], preferred_element_type=F32)
        return m_new, l, acc

    init = (jnp.full((tq, 1), -jnp.inf, F32), jnp.zeros((tq, 1), F32), jnp.zeros((tq, q_ref.shape[1]), F32))
    carry = lax.fori_loop(0, qi, lambda ki, cr: block(ki, cr, False), init)
    m, l, acc = block(qi, carry, True)
    o_ref[...] = (acc / l).astype(o_ref.dtype)


def _prompt_attention(q_p, k_p, v_p, lf_p, batch, seq, n_heads, head_dim, tq):
    m_p = q_p.shape[0]
    nq = seq // tq
    return pl.pallas_call(
        functools.partial(_pattn_kernel, tq=tq, seq=seq, scale=head_dim ** -0.5),
        grid=(batch, n_heads, nq),
        in_specs=[pl.BlockSpec((tq, head_dim), lambda b, h, qi: (b * nq + qi, h)),
                  pl.BlockSpec((seq, head_dim), lambda b, h, qi: (b, h)),
                  pl.BlockSpec((seq, head_dim), lambda b, h, qi: (b, h)),
                  pl.BlockSpec((seq, LANES), lambda b, h, qi: (b, 0))],
        out_specs=pl.BlockSpec((tq, head_dim), lambda b, h, qi: (b * nq + qi, h)),
        out_shape=jax.ShapeDtypeStruct((m_p, n_heads * head_dim), BF16),
        scratch_shapes=[pltpu.VMEM((seq, head_dim), BF16), pltpu.VMEM((seq, head_dim), BF16),
                        pltpu.VMEM((seq, LANES), F32), pltpu.VMEM((nq, LANES, tq), F32)],
        compiler_params=_params("arbitrary", "arbitrary", "arbitrary"),
        name="prompt_attention",
    )(q_p, k_p, v_p, lf_p)


def _sattn_kernel(pt_ref, wq_ref, kn_ref, vn_ref, lfn_ref, *refs, g_pages, page, n_heads, t_new, scale):
    k_refs = refs[:g_pages]
    v_refs = refs[g_pages:2 * g_pages]
    lf_refs = refs[2 * g_pages:3 * g_pages]
    o_ref = refs[3 * g_pages]
    kb, vb, lfbuf, rtbuf, crow_s, m_s, l_s, acc_s, carry_s = refs[3 * g_pages + 1:]
    st = pl.program_id(1)
    n_st = pl.num_programs(1)
    ht = n_heads * t_new
    hd = kb.shape[1]
    d = hd // n_heads
    wq = wq_ref[...]

    expand = (lax.broadcasted_iota(jnp.int32, (LANES, ht), 0)
              == lax.broadcasted_iota(jnp.int32, (LANES, ht), 1) // t_new).astype(BF16)
    key_h = lax.broadcasted_iota(jnp.int32, (n_heads, page), 1)
    key_q = lax.broadcasted_iota(jnp.int32, (page, ht), 0)
    lane_t = lax.broadcasted_iota(jnp.int32, (page, ht), 1) % t_new

    def update(s_t, v_chunk):
        m_old = m_s[0:1, :]
        m_new = jnp.maximum(m_old, jnp.max(s_t, axis=0, keepdims=True))
        a = jnp.exp(m_old - m_new)
        p = jnp.exp(s_t - m_new)
        l_s[...] = jnp.broadcast_to(a * l_s[0:1, :] + jnp.sum(p, axis=0, keepdims=True), l_s.shape)
        m_s[...] = jnp.broadcast_to(m_new, m_s.shape)
        o_full = jnp.dot(p.T.astype(BF16), v_chunk, preferred_element_type=F32)
        o_diag = jnp.concatenate(
            [o_full[hh * t_new:(hh + 1) * t_new, hh * d:(hh + 1) * d] for hh in range(n_heads)], axis=0)
        a_col = jnp.broadcast_to(a, (ht, ht)).T[:, 0:1]
        acc_s[...] = a_col * acc_s[...] + o_diag

    @pl.when(st == 0)
    def _():
        lfbuf[...] = jnp.zeros(lfbuf.shape, F32)
        rtbuf[...] = jnp.zeros(rtbuf.shape, F32)
        carry_s[...] = jnp.zeros(carry_s.shape, F32)
        m_s[...] = jnp.full(m_s.shape, -jnp.inf, F32)
        l_s[...] = jnp.zeros(l_s.shape, F32)
        acc_s[...] = jnp.zeros(acc_s.shape, F32)
        pad = jnp.zeros((page - t_new, hd), F32)
        kn = jnp.concatenate([kn_ref[...], pad], axis=0).astype(BF16)
        vn = jnp.concatenate([vn_ref[...], pad], axis=0).astype(BF16)
        lfbuf[0:t_new, 0:n_heads] = lfn_ref[...]
        tri = (lax.broadcasted_iota(jnp.int32, (page, page), 0)
               >= lax.broadcasted_iota(jnp.int32, (page, page), 1)).astype(BF16)
        c_nat = _dot01_left(tri, lfbuf[0:page, :])
        c_exp = _dot01_right(c_nat, expand)
        c_row = jnp.sum(jnp.where(key_q == lane_t, c_exp, 0.0), axis=0, keepdims=True)
        crow_s[...] = jnp.broadcast_to(c_row, crow_s.shape)
        s_t = jnp.dot(kn, wq, preferred_element_type=F32) * scale
        s_t = s_t + (c_row - c_exp)
        s_t = jnp.where(key_q <= lane_t, s_t, -jnp.inf)
        update(s_t, vn)

    c_row = crow_s[0:1, :]
    for g in range(g_pages):
        for hh in range(n_heads):
            kb[g * page:(g + 1) * page, hh * d:(hh + 1) * d] = (
                k_refs[g][pl.ds(hh, page, stride=n_heads), :].astype(BF16))
            vb[g * page:(g + 1) * page, hh * d:(hh + 1) * d] = (
                v_refs[g][pl.ds(hh, page, stride=n_heads), :].astype(BF16))
    carry = carry_s[...]
    for g in reversed(range(g_pages)):
        y = lf_refs[g][...]
        shift = 1
        while shift < page:
            y = y + jnp.where(key_h + shift < page, pltpu.roll(y, page - shift, axis=1), 0.0)
            shift *= 2
        excl = jnp.where(key_h + 1 < page, pltpu.roll(y, page - 1, axis=1), 0.0)
        rtbuf[g, 0:n_heads, :] = excl + carry
        carry = carry + jnp.broadcast_to(y[:, 0:1], carry.shape)
    carry_s[...] = carry
    r_nat = jnp.concatenate([rtbuf[g].T for g in range(g_pages)], axis=0)
    r_exp = _dot01_right(r_nat, expand)
    s_t = jnp.dot(kb[...], wq, preferred_element_type=F32) * scale
    s_t = s_t + (c_row + r_exp)
    update(s_t, vb[...])

    @pl.when(st == n_st - 1)
    def _():
        l_col = jnp.broadcast_to(l_s[0:1, :], (ht, ht)).T[:, 0:1]
        o_ref[...] = acc_s[...] / l_col


def _sample_attention(page_table, wq, kn, vn, lfn, cache_k, cache_v, cache_logf, layer, g_pages,
                      n_heads, head_dim, t_new):
    bs, n_pages = page_table.shape
    page = cache_k.shape[2]
    hd = n_heads * head_dim
    ht = n_heads * t_new
    n_st = n_pages // g_pages
    ck = cache_k.reshape(cache_k.shape[0], cache_k.shape[1], page * n_heads, head_dim)
    cv = cache_v.reshape(cache_v.shape[0], cache_v.shape[1], page * n_heads, head_dim)
    clf = cache_logf.transpose(0, 1, 3, 2)

    def page_map(g):
        def f(b, st, pt):
            return (layer, pt[b * n_pages + (n_st - 1 - st) * g_pages + g], 0, 0)
        return f

    in_specs = [pl.BlockSpec((None, hd, ht), lambda b, st, pt: (b, 0, 0)),
                pl.BlockSpec((None, t_new, hd), lambda b, st, pt: (b, 0, 0)),
                pl.BlockSpec((None, t_new, hd), lambda b, st, pt: (b, 0, 0)),
                pl.BlockSpec((None, t_new, n_heads), lambda b, st, pt: (b, 0, 0))]
    in_specs += [pl.BlockSpec((None, None, page * n_heads, head_dim), page_map(g)) for g in range(g_pages)]
    in_specs += [pl.BlockSpec((None, None, page * n_heads, head_dim), page_map(g)) for g in range(g_pages)]
    in_specs += [pl.BlockSpec((None, None, n_heads, page), page_map(g)) for g in range(g_pages)]
    grid_spec = pltpu.PrefetchScalarGridSpec(
        num_scalar_prefetch=1, grid=(bs, n_st), in_specs=in_specs,
        out_specs=pl.BlockSpec((None, ht, head_dim), lambda b, st, pt: (b, 0, 0)),
        scratch_shapes=[pltpu.VMEM((g_pages * page, hd), BF16), pltpu.VMEM((g_pages * page, hd), BF16),
                        pltpu.VMEM((page, LANES), F32), pltpu.VMEM((g_pages, LANES, page), F32),
                        pltpu.VMEM((8, ht), F32), pltpu.VMEM((8, ht), F32), pltpu.VMEM((8, ht), F32),
                        pltpu.VMEM((ht, head_dim), F32), pltpu.VMEM((n_heads, page), F32)])
    return pl.pallas_call(
        functools.partial(_sattn_kernel, g_pages=g_pages, page=page, n_heads=n_heads, t_new=t_new,
                          scale=head_dim ** -0.5),
        grid_spec=grid_spec,
        out_shape=jax.ShapeDtypeStruct((bs, ht, head_dim), F32),
        compiler_params=_params("arbitrary", "arbitrary"),
        name="sample_attention",
    )(page_table.reshape(-1), wq, kn, vn, lfn, *([ck] * g_pages), *([cv] * g_pages),
      *([clf] * g_pages))


def _conv_rows(window, w_ref, bdw, gcn, bcn, width, group):
    ch = w_ref.shape[1]
    parts = []
    for c0 in range(0, ch, group):
        acc = None
        for j in range(width):
            term = w_ref[j:j + 1, c0:c0 + group] * window(j, c0, c0 + group)
            acc = term if acc is None else acc + term
        parts.append(acc)
    y = jnp.concatenate(parts, axis=1) + bdw
    mu = jnp.mean(y, axis=-1, keepdims=True)
    yc = y - mu
    var = jnp.mean(yc * yc, axis=-1, keepdims=True)
    z = yc * lax.rsqrt(var + LN_EPS) * gcn + bcn
    return z * jax.nn.sigmoid(z)


def _pconv_kernel(u_ref, halo_ref, w_ref, bdw_ref, gcn_ref, bcn_ref, o_ref, ext, *, tt, halo, width, chunk,
                  group):
    ti = pl.program_id(1)
    n_ext = halo + tt
    ext[0, 0:halo, :] = jnp.where(ti > 0, halo_ref[...], 0.0)
    ext[0, halo:n_ext, :] = u_ref[...]
    for r in range(1, SUBLANES):
        ext[r, 0:n_ext - SUBLANES, :] = ext[0, r:r + n_ext - SUBLANES, :]
    bdw, gcn, bcn = bdw_ref[...], gcn_ref[...], bcn_ref[...]
    first = halo - (width - 1)

    def body(c, _):
        r0 = pl.multiple_of(c * chunk, chunk)

        def window(j, c0, c1):
            off = first + j
            return ext[off % SUBLANES, pl.ds(r0 + (off // SUBLANES) * SUBLANES, chunk), c0:c1]

        z = _conv_rows(window, w_ref, bdw, gcn, bcn, width, group)
        o_ref[pl.ds(r0, chunk), :] = z.astype(o_ref.dtype)
        return 0

    lax.fori_loop(0, tt // chunk, body, 0)


def _prompt_conv(u_p, w_dw, b_dw, g_cn, b_cn, layer, batch, seq, tt, chunk, group):
    m_p, ch = u_p.shape
    width = w_dw.shape[1]
    halo = 32
    assert width - 1 <= halo and tt % halo == 0
    nt = seq // tt
    vec = pl.BlockSpec((None, 1, ch), lambda b, ti: (layer, 0, 0))
    return pl.pallas_call(
        functools.partial(_pconv_kernel, tt=tt, halo=halo, width=width, chunk=chunk, group=group),
        grid=(batch, nt),
        in_specs=[pl.BlockSpec((tt, ch), lambda b, ti: (b * nt + ti, 0)),
                  pl.BlockSpec((halo, ch), lambda b, ti: (jnp.maximum((b * nt + ti) * (tt // halo) - 1, 0), 0)),
                  pl.BlockSpec((None, width, ch), lambda b, ti: (layer, 0, 0)),
                  vec, vec, vec],
        out_specs=pl.BlockSpec((tt, ch), lambda b, ti: (b * nt + ti, 0)),
        out_shape=jax.ShapeDtypeStruct((m_p, ch), BF16),
        scratch_shapes=[pltpu.VMEM((SUBLANES, halo + tt, ch), F32)],
        compiler_params=_params("arbitrary", "arbitrary"),
        name="prompt_conv",
    )(u_p, u_p, w_dw, b_dw, g_cn, b_cn)


def _sconv_kernel(ext_ref, w_ref, bdw_ref, gcn_ref, bcn_ref, o_ref, *, n_seq, t_new, width):
    bdw, gcn, bcn = bdw_ref[...], gcn_ref[...], bcn_ref[...]
    for b in range(n_seq):
        z = _conv_rows(lambda j, c0, c1, b=b: ext_ref[b, j:j + t_new, c0:c1], w_ref, bdw, gcn, bcn, width,
                       w_ref.shape[1])
        o_ref[b * t_new:(b + 1) * t_new, :] = z


def _sample_conv(ext_s, w_dw, b_dw, g_cn, b_cn, layer, t_new):
    n_seq, rows, ch = ext_s.shape
    width = w_dw.shape[1]
    vec = pl.BlockSpec((None, 1, ch), lambda i: (layer, 0, 0))
    return pl.pallas_call(
        functools.partial(_sconv_kernel, n_seq=n_seq, t_new=t_new, width=width),
        grid=(1,),
        in_specs=[pl.BlockSpec((n_seq, rows, ch), lambda i: (0, 0, 0)),
                  pl.BlockSpec((None, width, ch), lambda i: (layer, 0, 0)),
                  vec, vec, vec],
        out_specs=pl.BlockSpec((n_seq * t_new, ch), lambda i: (0, 0)),
        out_shape=jax.ShapeDtypeStruct((n_seq * t_new, ch), F32),
        compiler_params=_params("arbitrary"),
        name="sample_conv",
    )(ext_s, w_dw, b_dw, g_cn, b_cn)


def _largest_divisor(n, cap, multiple):
    best = None
    for t in range(multiple, min(n, cap) + 1, multiple):
        if n % t == 0:
            best = t
    assert best is not None, (n, cap, multiple)
    return best


def kernel(x_prompt, x_sample, c_prompt, c_sample, cache_k, cache_v, cache_logf, state_conv, page_table,
           w_ada, b_ada, w_ffn1_in, w_ffn1_down, w_in, b_f, w_dw, b_dw, g_cn, b_cn, w_out,
           w_ffn2_in, w_ffn2_down, ln_g, ln_b):
    batch, seq, d = x_prompt.shape
    bs, t_new, _ = x_sample.shape
    depth = w_ada.shape[0]
    n_heads, head_dim = cache_k.shape[3], cache_k.shape[4]
    aw = n_heads * head_dim
    ch = d - aw
    d_ff = w_ffn1_down.shape[1]
    width = w_dw.shape[1]
    alpha = (2 * depth) ** 0.25
    m_p, m_s = batch * seq, bs * t_new

    tm = _largest_divisor(m_p, 512, 16)
    tr = _largest_divisor(seq, 256, 16)
    tn_ff = _largest_divisor(d_ff, 256, LANES)
    tn_aw = _largest_divisor(aw, 256, LANES)
    tn_ch = _largest_divisor(ch, 256, LANES)
    tn_d = _largest_divisor(d, 512, LANES)
    nb_ff, nb_aw, nb_ch = d_ff // tn_ff, aw // tn_aw, ch // tn_ch
    k_split = 2 if d_ff % (2 * LANES) == 0 else 1
    tn_ada = _largest_divisor(N_MOD * d, 512, LANES)
    tq = _largest_divisor(seq, 1024, LANES)
    tt = _largest_divisor(seq, 256, 32)
    conv_chunk = _largest_divisor(tt, 64, 16)
    conv_group = _largest_divisor(ch, 128, LANES)
    g_pages = _largest_divisor(page_table.shape[1], 8, 1)

    n_seq_all = batch + bs
    c_all = jnp.concatenate([c_prompt, c_sample, jnp.zeros((-n_seq_all % 8, d), F32)], axis=0)
    b_ada3 = b_ada.reshape(depth, 1, N_MOD * d)
    mods = []
    for l in range(depth):
        m = _adaln(c_all, w_ada, b_ada3, l, tn_ada)
        mod_p = m[:batch].reshape(batch, N_MOD, 1, d)
        mod_s = jnp.repeat(m[batch:n_seq_all].reshape(bs, N_MOD, d).transpose(1, 0, 2), t_new, axis=1)
        mods.append((mod_p, mod_s))

    c_f = 3 * aw
    c_a = c_f + n_heads
    c_gl = c_a + ch
    assert c_f % LANES == 0 and n_heads < LANES
    w_in_t = w_in.transpose(0, 2, 1)
    b_f3 = jnp.pad(b_f, ((0, 0), (0, LANES - n_heads))).reshape(depth, 1, LANES)
    w_a_t = _shift_rows(w_in_t, c_a, ch, tn_ch)
    w_gl_t = _shift_rows(w_in_t, c_gl, ch, tn_ch)
    ln_g4 = ln_g.reshape(depth, 3, 1, d)
    ln_b4 = ln_b.reshape(depth, 3, 1, d)
    b_dw3, g_cn3, b_cn3 = (v.reshape(depth, 1, ch) for v in (b_dw, g_cn, b_cn))
    eye_h = jnp.eye(n_heads, dtype=bool)

    x = (x_prompt.reshape(m_p, d), x_sample.reshape(m_s, d))
    xm = _modulate(x[0], x[1], mods[0][0], mods[0][1], 0, tr, seq)

    def cols(w, first, n_blocks, pieces):
        return [[_w(w, col_off=first + q, col_mul=pieces, col_max=first + n_blocks - 1)] for q in range(pieces)]

    def ffn(x, xm, w_in_l, w_down_l, l, which_gate, which_ln, next_mod):
        (h,) = _mm([(xm, d, 0)], cols(w_in_l, 0, nb_ff, 2) + cols(w_in_l, nb_ff, nb_ff, 2), tn_ff, 2 * tn_ff,
                   d_ff, _epi_swiglu, [BF16], tm, l, name="ffn_in")
        y = None
        for kb in range(k_split):
            (y,) = _mm([(h, d_ff // k_split, kb)], [[_w(w_down_l, row_block=kb)]], tn_d, tn_d, d, _epi_plain,
                       [BF16 if kb == k_split - 1 else F32], tm, l, add=y, name="ffn_down")
        return _ln_mod(x, y, mods[l][0], mods[l][1], which_gate, 0.5, alpha, ln_g4, ln_b4, l, which_ln,
                       next_mod, tr, seq)

    outs = {k: [] for k in ("kp", "vp", "lfp", "cvp", "ks", "vs", "lfs", "cvs")}
    for l in range(depth):
        mod_p, mod_s = mods[l]
        x, xm = ffn(x, xm, w_ffn1_in, w_ffn1_down, l, 2, 0, (mod_p, mod_s, 3))

        xin = [(xm, d, 0)]
        (q,) = _mm(xin, cols(w_in_t, 0, nb_aw, 4), tn_aw, 4 * tn_aw, aw, _epi_plain, [BF16], tm, l, w_t=True,
                   name="proj_q")
        (k,) = _mm(xin, cols(w_in_t, nb_aw, nb_aw, 4), tn_aw, 4 * tn_aw, aw, _epi_plain, [F32], tm, l, w_t=True,
                   name="proj_k")
        (v,) = _mm(xin, cols(w_in_t, 2 * nb_aw, nb_aw, 4), tn_aw, 4 * tn_aw, aw, _epi_plain, [F32], tm, l,
                   w_t=True, name="proj_v")
        (lf,) = _mm(xin, [[_w(w_in_t, col_off=c_f // LANES)]], LANES, LANES, LANES, _epi_logsig, [F32], tm, l,
                    bias=b_f3, w_t=True, name="proj_f")
        (u,) = _mm(xin, cols(w_a_t, 0, nb_ch, 2) + cols(w_gl_t, 0, nb_ch, 2), tn_ch, 2 * tn_ch, ch, _epi_glu,
                   [F32], tm, l, w_t=True, name="proj_glu")

        attn_p = _prompt_attention(q[0], k[0], v[0], lf[0], batch, seq, n_heads, head_dim, tq)
        conv_p = _prompt_conv(u[0], w_dw, b_dw3, g_cn3, b_cn3, l, batch, seq, tt, conv_chunk, conv_group)

        q_s = q[1].reshape(bs, t_new, n_heads, head_dim).transpose(0, 2, 3, 1)
        wq = jnp.where(eye_h[None, :, None, :, None], q_s[:, :, :, None, :], 0).reshape(
            bs, aw, n_heads * t_new)
        lfn = lf[1][:, :n_heads].reshape(bs, t_new, n_heads)
        o_s = _sample_attention(page_table, wq, k[1].reshape(bs, t_new, aw), v[1].reshape(bs, t_new, aw),
                                lfn, cache_k, cache_v, cache_logf, l, g_pages, n_heads, head_dim, t_new)
        attn_s = o_s.reshape(bs, n_heads, t_new, head_dim).transpose(0, 2, 1, 3).reshape(m_s, aw).astype(BF16)
        u_s = u[1].reshape(bs, t_new, ch)
        ext_s = jnp.concatenate([state_conv[l], u_s, jnp.zeros((bs, -(width - 1 + t_new) % 8, ch), F32)],
                                axis=1)
        conv_s = _sample_conv(ext_s, w_dw, b_dw3, g_cn3, b_cn3, l, t_new).astype(BF16)

        assert aw == ch
        nb_d = d // tn_d
        (y,) = _mm([((attn_p, attn_s), aw, 0), ((conv_p, conv_s), ch, 0)],
                   [[_w(w_out, row_block=0, col_off=q, col_mul=2, col_max=nb_d - 1),
                     _w(w_out, row_block=1, col_off=q, col_mul=2, col_max=nb_d - 1)] for q in range(2)],
                   tn_d, 2 * tn_d, d, _epi_plain, [BF16], tm, l, name="proj_out")
        x, xm = _ln_mod(x, y, mod_p, mod_s, 5, 1.0, alpha, ln_g4, ln_b4, l, 1, (mod_p, mod_s, 6), tr, seq)

        next_mod = (mods[l + 1][0], mods[l + 1][1], 0) if l + 1 < depth else None
        x, xm = ffn(x, xm, w_ffn2_in, w_ffn2_down, l, 8, 2, next_mod)

        outs["kp"].append(k[0].reshape(batch, seq, n_heads, head_dim))
        outs["vp"].append(v[0].reshape(batch, seq, n_heads, head_dim))
        outs["lfp"].append(lf[0][:, :n_heads].reshape(batch, seq, n_heads))
        outs["cvp"].append(u[0].reshape(batch, seq, ch)[:, seq - (width - 1):])
        outs["ks"].append(k[1].reshape(bs, t_new, n_heads, head_dim))
        outs["vs"].append(v[1].reshape(bs, t_new, n_heads, head_dim))
        outs["lfs"].append(lfn)
        outs["cvs"].append(jnp.concatenate([state_conv[l], u_s], axis=1)[:, -(width - 1):])

    return (x[0].reshape(batch, seq, d), x[1].reshape(bs, t_new, d),
            jnp.stack(outs["kp"]), jnp.stack(outs["vp"]), jnp.stack(outs["lfp"]), jnp.stack(outs["cvp"]),
            jnp.stack(outs["ks"]), jnp.stack(outs["vs"]), jnp.stack(outs["lfs"]), jnp.stack(outs["cvs"]))
```
